```python
import functools
import jax, jax.numpy as jnp
from jax import lax
import numpy as np

D_MODEL = 1024
BATCH = 16
SEQ = 256
DEPTH = 2
DEC_BATCH = 8
DEC_SEQ = 4096
PAST_LEN = 512

GRID_W = 64
NA_HEAD_DIM = 64
NA_WIDTH = D_MODEL // 2
NA_HEADS = NA_WIDTH // NA_HEAD_DIM
NA_KH_MAX = 8
NA_KW = 16
NA_QB = 16
NA_KC = min(GRID_W, NA_QB + NA_KW)
HG_DK = 64
HG_DV = 64
HG_WIDTH = D_MODEL // 4
HG_HEADS = HG_WIDTH // HG_DK
HG_CHUNK = 64
POOL_WINDOWS = (2, 4, 8, 16)
POOL_GROUPS = len(POOL_WINDOWS)
POOL_WIDTH = D_MODEL // 4
POOL_GROUP_DIM = POOL_WIDTH // POOL_GROUPS
MIX_WIDTH = NA_WIDTH + HG_WIDTH + POOL_WIDTH
IN_WIDTH = 3 * NA_WIDTH + 5 * HG_WIDTH + POOL_WIDTH
N_EXPERTS = 16
EXPERT_FF = 1024
EC_CAPACITY_FACTOR = 2
CTX_BLOCK = 128
EPS = 1e-6
NEG_BIG = -1e30

kernel_name = 'hybrid_na_hgrn2_pool_ec_diffusion_step'


def rms_norm(x, g):
    xf = x.astype(jnp.float32)
    y = xf * lax.rsqrt(jnp.mean(jnp.square(xf), axis=-1, keepdims=True) + EPS)
    return (y * g.astype(jnp.float32)).astype(x.dtype)


def _modulation(cond, ada_w_l, ada_b_l):
    m = jax.nn.silu(cond) @ ada_w_l + ada_b_l
    return jnp.split(m[:, None, :], 6, axis=-1)


def _split_proj(p):
    sizes = [NA_WIDTH] * 3 + [HG_WIDTH] * 5 + [POOL_WIDTH]
    return jnp.split(p, np.cumsum(sizes)[:-1].tolist(), axis=-1)


def _dense_ctx_attention(q, k, v):
    B, M, H, hd = q.shape
    nb = M // CTX_BLOCK
    qb = (q * hd ** -0.5).reshape(B, nb, CTX_BLOCK, H, hd).swapaxes(0, 1)

    def blk(qi):
        s = jnp.einsum('bqhd,bmhd->bhqm', qi, k).astype(jnp.float32)
        p = jax.nn.softmax(s, axis=-1).astype(v.dtype)
        return jnp.einsum('bhqm,bmhd->bqhd', p, v)

    o = lax.map(blk, qb)
    return o.swapaxes(0, 1).reshape(B, M, H * hd)


def _na_col_tables():
    nqb = GRID_W // NA_QB
    qc = np.arange(nqb)[:, None] * NA_QB + np.arange(NA_QB)[None]
    kstart = np.clip(np.arange(nqb) * NA_QB - NA_KW // 2, 0, GRID_W - NA_KC)
    kc = kstart[:, None] + np.arange(NA_KC)[None]
    ws = np.clip(qc - NA_KW // 2, 0, GRID_W - NA_KW)
    mask = (kc[:, None, :] >= ws[..., None]) & (kc[:, None, :] < ws[..., None] + NA_KW)
    dc = np.clip(kc[:, None, :] - qc[..., None] + NA_KW - 1, 0, 2 * NA_KW - 2)
    return kc, mask, dc


def _na_latent(q, k, v, ctx_k, ctx_v, rpb):
    B, N, H, hd = q.shape
    rows = N // GRID_W
    kh = min(NA_KH_MAX, rows)
    nqb = GRID_W // NA_QB
    col_idx, mask, dc_idx = _na_col_tables()
    mask = jnp.asarray(mask)[:, :, None, :]
    qg = (q * hd ** -0.5).reshape(B, rows, nqb, NA_QB, H, hd)
    kg = k.reshape(B, rows, GRID_W, H, hd)
    vg = v.reshape(B, rows, GRID_W, H, hd)
    bias_c = rpb[:, :, dc_idx]

    def row_fn(r):
        rs = jnp.clip(r - kh // 2, 0, rows - kh)
        k_rows = lax.dynamic_slice_in_dim(kg, rs, kh, axis=1)[:, :, col_idx]
        v_rows = lax.dynamic_slice_in_dim(vg, rs, kh, axis=1)[:, :, col_idx]
        q_r = lax.dynamic_index_in_dim(qg, r, axis=1, keepdims=False)
        dr_idx = rs + jnp.arange(kh) - r + NA_KH_MAX - 1
        bias = jnp.take(bias_c, dr_idx, axis=1).transpose(0, 2, 3, 1, 4)
        s_loc = jnp.einsum('bjqhd,bijkhd->bhjqik', q_r, k_rows).astype(jnp.float32) + bias
        s_loc = jnp.where(mask, s_loc, NEG_BIG)
        s_ctx = jnp.einsum('bjqhd,bmhd->bhjqm', q_r, ctx_k).astype(jnp.float32)
        n_loc = kh * NA_KC
        s = jnp.concatenate([s_loc.reshape(B, H, nqb, NA_QB, n_loc), s_ctx], axis=-1)
        p = jax.nn.softmax(s, axis=-1).astype(v.dtype)
        p_loc = p[..., :n_loc].reshape(B, H, nqb, NA_QB, kh, NA_KC)
        o = (jnp.einsum('bhjqik,bijkhd->bjqhd', p_loc, v_rows)
             + jnp.einsum('bhjqm,bmhd->bjqhd', p[..., n_loc:], ctx_v))
        return o.reshape(B, GRID_W, H * hd)

    o = lax.map(row_fn, jnp.arange(rows))
    return o.swapaxes(0, 1).reshape(B, N, H * hd)


def _hgrn_gate(f_lin, lb):
    B, N, _ = f_lin.shape
    log_f = jnp.logaddexp(jnp.log(lb), jnp.log1p(-lb) + jax.nn.log_sigmoid(f_lin.astype(jnp.float32)))
    k = -jnp.expm1(log_f)
    return k.reshape(B, N, HG_HEADS, HG_DK), log_f.reshape(B, N, HG_HEADS, HG_DK)


def _hgrn_chunk_scan(q, k, v, log_f, s0):
    B, N, H, DK = q.shape
    DV = v.shape[-1]
    nc = N // HG_CHUNK

    def to_chunks(a):
        return a.reshape(B, nc, HG_CHUNK, H, a.shape[-1]).swapaxes(0, 1)

    causal = jnp.tril(jnp.ones((HG_CHUNK, HG_CHUNK), dtype=bool))[None, :, :, None, None]

    def step(S, inp):
        qc, kc, vc, lf = inp
        b = jnp.cumsum(lf, axis=1)
        diff = b[:, :, None] - b[:, None, :]
        decay = jnp.exp(jnp.where(causal, diff, -jnp.inf))
        a = jnp.einsum('bthk,bshk,btshk->bhts', qc, kc, decay)
        o = (jnp.einsum('bhts,bshv->bthv', a, vc)
             + jnp.einsum('bthk,bhkv->bthv', qc * jnp.exp(b), S))
        b_last = b[:, -1]
        S_new = (jnp.exp(b_last)[..., None] * S
                 + jnp.einsum('bshk,bshv->bhkv', kc * jnp.exp(b_last[:, None] - b), vc))
        return S_new, o

    s_fin, o = lax.scan(step, s0, (to_chunks(q), to_chunks(k), to_chunks(v), to_chunks(log_f)))
    return o.swapaxes(0, 1).reshape(B, N, H, DV), s_fin


def _pool_mixer(a, pool_w_l, pool_scale_l):
    B, N, _ = a.shape
    af = a.astype(jnp.float32).reshape(B, N, POOL_GROUPS, POOL_GROUP_DIM)
    cs = jnp.pad(jnp.cumsum(af, axis=1), ((0, 0), (1, 0), (0, 0), (0, 0)))
    t = jnp.arange(N)
    pooled = []
    for gi, w in enumerate(POOL_WINDOWS):
        lo = jnp.clip(t - w // 2, 0, N)
        hi = jnp.clip(t + w // 2, 0, N)
        win_sum = cs[:, hi, gi] - cs[:, lo, gi]
        pooled.append(win_sum / (hi - lo).astype(jnp.float32)[:, None] - af[:, :, gi])
    p = jnp.stack(pooled, axis=2)
    y = jnp.einsum('bngc,gcd->bngd', p, pool_w_l.astype(jnp.float32)).reshape(B, N, POOL_WIDTH)
    return (y * pool_scale_l.astype(jnp.float32)).astype(a.dtype)


def _token_mixers(u, w_in_l, lb_l, hg_norm_l, pool_w_l, pool_scale_l, attend, s0_f, s0_b):
    B, N, _ = u.shape
    na_q, na_k, na_v, hg_q, hg_ff, hg_fb, hg_i, hg_g, pool_in = _split_proj(u @ w_in_l)

    def heads(a, h):
        return a.reshape(B, N, h, -1)

    k_na = heads(na_k, NA_HEADS)
    v_na = heads(na_v, NA_HEADS)
    o_na = attend(heads(na_q, NA_HEADS), k_na, v_na)

    q = heads(jax.nn.silu(hg_q.astype(jnp.float32)), HG_HEADS)
    v = heads(hg_i.astype(jnp.float32), HG_HEADS)
    k_f, lf_f = _hgrn_gate(hg_ff, lb_l[0])
    k_b, lf_b = _hgrn_gate(hg_fb, lb_l[1])
    o_f, s_f = _hgrn_chunk_scan(q, k_f, v, lf_f, s0_f)
    o_b, s_b = _hgrn_chunk_scan(q[:, ::-1], k_b[:, ::-1], v[:, ::-1], lf_b[:, ::-1], s0_b)
    o_hg = rms_norm(o_f + o_b[:, ::-1], hg_norm_l.reshape(HG_HEADS, HG_DV))
    o_hg = (o_hg * jax.nn.silu(heads(hg_g.astype(jnp.float32), HG_HEADS))).reshape(B, N, HG_WIDTH)

    o_pool = _pool_mixer(pool_in, pool_w_l, pool_scale_l)
    mix = jnp.concatenate([o_na, o_hg.astype(u.dtype), o_pool], axis=-1)
    return mix, k_na, v_na, s_f, s_b


def _expert_choice_ffn(u, w_router_l, w_gate_l, w_up_l, w_down_l):
    B, N, D = u.shape
    cap = EC_CAPACITY_FACTOR * N // N_EXPERTS
    aff = jax.nn.softmax(jnp.einsum('bnd,de->ben', u, w_router_l).astype(jnp.float32), axis=1)
    gate, idx = lax.top_k(aff, cap)
    xs = jax.vmap(lambda ub, ib: ub[ib])(u, idx)
    hid = jax.nn.silu(jnp.einsum('becd,edf->becf', xs, w_gate_l)) * jnp.einsum('becd,edf->becf', xs, w_up_l)
    out = jnp.einsum('becf,efd->becd', hid, w_down_l) * gate[..., None].astype(u.dtype)
    return jax.vmap(lambda ob, ib: jnp.zeros((N, D), ob.dtype).at[ib.reshape(-1)].add(ob.reshape(-1, D)))(out, idx)


def setup_inputs(seed: int = 0) -> dict:
    key = jax.random.key(seed)
    ks = jax.random.split(key, 24)
    D = D_MODEL

    def nrm(k, shape, s=1.0):
        return jax.random.normal(k, shape, jnp.float32) * s

    return {
        'x_prompt': nrm(ks[0], (BATCH, SEQ, D)),
        'x_sample': nrm(ks[1], (DEC_BATCH, DEC_SEQ, D)),
        'cache_na_k': nrm(ks[2], (DEC_BATCH, DEPTH, PAST_LEN, NA_HEADS, NA_HEAD_DIM)),
        'cache_na_v': nrm(ks[3], (DEC_BATCH, DEPTH, PAST_LEN, NA_HEADS, NA_HEAD_DIM)),
        'state_hgrn': nrm(ks[4], (DEC_BATCH, DEPTH, 2, HG_HEADS, HG_DK, HG_DV), 0.5),
        'c': nrm(ks[5], (DEC_BATCH, D)),
        'c_ctx': nrm(ks[6], (D,)),
        'ada_w': nrm(ks[7], (DEPTH, D, 6 * D), 0.5 * D ** -0.5),
        'ada_b': nrm(ks[8], (DEPTH, 6 * D), 0.02),
        'norm_pre_mix': 1.0 + nrm(ks[9], (DEPTH, D), 0.05),
        'norm_post_mix': 1.0 + nrm(ks[10], (DEPTH, D), 0.05),
        'norm_pre_ffn': 1.0 + nrm(ks[11], (DEPTH, D), 0.05),
        'norm_post_ffn': 1.0 + nrm(ks[12], (DEPTH, D), 0.05),
        'w_in': nrm(ks[13], (DEPTH, D, IN_WIDTH), D ** -0.5),
        'w_out': nrm(ks[14], (DEPTH, MIX_WIDTH, D), MIX_WIDTH ** -0.5),
        'na_rpb': nrm(ks[15], (DEPTH, NA_HEADS, 2 * NA_KH_MAX - 1, 2 * NA_KW - 1), 0.5),
        'hg_lb': nrm(ks[16], (DEPTH, 2, HG_WIDTH), 0.5),
        'hg_norm': 1.0 + nrm(ks[17], (DEPTH, HG_WIDTH), 0.05),
        'pool_w': nrm(ks[18], (DEPTH, POOL_GROUPS, POOL_GROUP_DIM, POOL_GROUP_DIM), POOL_GROUP_DIM ** -0.5),
        'pool_scale': 1.0 + nrm(ks[19], (DEPTH, POOL_WIDTH), 0.05),
        'w_router': nrm(ks[20], (DEPTH, D, N_EXPERTS), D ** -0.5),
        'w_gate': nrm(ks[21], (DEPTH, N_EXPERTS, D, EXPERT_FF), D ** -0.5),
        'w_up': nrm(ks[22], (DEPTH, N_EXPERTS, D, EXPERT_FF), D ** -0.5),
        'w_down': nrm(ks[23], (DEPTH, N_EXPERTS, EXPERT_FF, D), EXPERT_FF ** -0.5),
    }


def reference(x_prompt, x_sample, cache_na_k, cache_na_v, state_hgrn, c, c_ctx, ada_w, ada_b,
              norm_pre_mix, norm_post_mix, norm_pre_ffn, norm_post_ffn, w_in, w_out, na_rpb,
              hg_lb, hg_norm, pool_w, pool_scale, w_router, w_gate, w_up, w_down):
    lb_all = jnp.cumsum(jax.nn.softmax(hg_lb.astype(jnp.float32), axis=0), axis=0)
    lb_all = lb_all - lb_all[:1]

    def layer(x, cond, l, attend, s0_f, s0_b):
        sh1, sc1, g1, sh2, sc2, g2 = _modulation(cond, ada_w[l], ada_b[l])
        u = rms_norm(x, norm_pre_mix[l]) * (1 + sc1) + sh1
        mix, k, v, s_f, s_b = _token_mixers(u, w_in[l], lb_all[l], hg_norm[l], pool_w[l], pool_scale[l],
                                            attend, s0_f, s0_b)
        x = x + g1 * rms_norm(mix @ w_out[l], norm_post_mix[l])
        u = rms_norm(x, norm_pre_ffn[l]) * (1 + sc2) + sh2
        ff = _expert_choice_ffn(u, w_router[l], w_gate[l], w_up[l], w_down[l])
        x = x + g2 * rms_norm(ff, norm_post_ffn[l])
        return x, k, v, s_f, s_b

    x = x_prompt
    zero_state = jnp.zeros((x_prompt.shape[0], HG_HEADS, HG_DK, HG_DV), jnp.float32)
    ks, vs, sts = [], [], []
    for l in range(DEPTH):
        x, k, v, s_f, s_b = layer(x, c_ctx[None], l, _dense_ctx_attention, zero_state, zero_state)
        ks.append(k)
        vs.append(v)
        sts.append(jnp.stack([s_f, s_b], axis=1))
    y_prompt = x
    new_cache_na_k = jnp.stack(ks, axis=1)
    new_cache_na_v = jnp.stack(vs, axis=1)
    new_state_hgrn = jnp.stack(sts, axis=1).astype(x_prompt.dtype)

    x = x_sample
    for l in range(DEPTH):
        attend = functools.partial(_na_latent, ctx_k=cache_na_k[:, l], ctx_v=cache_na_v[:, l], rpb=na_rpb[l])
        x, _, _, _, _ = layer(x, c, l, attend,
                              state_hgrn[:, l, 0].astype(jnp.float32), state_hgrn[:, l, 1].astype(jnp.float32))
    y_sample = x
    return (y_prompt, y_sample, new_cache_na_k, new_cache_na_v, new_state_hgrn)
```

```python
import functools

import numpy as np
import jax
import jax.numpy as jnp
from jax import lax
from jax.experimental import pallas as pl
from jax.experimental.pallas import tpu as pltpu

F32 = jnp.float32
BF16 = jnp.bfloat16
I32 = jnp.int32

HEAD_DIM = 64
LOG_HEAD_DIM = 6
LANES = 128
GRID_W = 64
NA_KH = 8
NA_KW = 16
NA_RB = 8
NA_WR = 16
HG_CHUNK = 64
HG_SUB = 16
HG_CLAMP = 80.0
POOL_HALF = (1, 2, 4, 8)
N_EXPERTS = 16
EPS = 1e-6
NEG_BIG = -1e30
TOK_TILE = 512
VMEM_LIMIT = 56 * 1024 * 1024


def _cparams(sem, vmem=None):
    return pltpu.CompilerParams(dimension_semantics=sem, vmem_limit_bytes=vmem)


def _silu(x):
    return x / (1.0 + jnp.exp(-x))


def _dot(a, b):
    return jnp.dot(a, b, preferred_element_type=F32)


def _dot_nt(a, b):
    return lax.dot_general(a, b, (((1,), (1,)), ((), ())), preferred_element_type=F32)


def _dot_tn(a, b):
    return lax.dot_general(a, b, (((0,), (0,)), ((), ())), preferred_element_type=F32)


def _rms(x):
    return x * lax.rsqrt(jnp.mean(x * x, axis=-1, keepdims=True) + EPS)


def _store_token_tiles(ref, x):
    n, d = x.shape
    nchunk = d // LANES
    for c in range(nchunk):
        ref[0, pl.ds(c, n, stride=nchunk), :] = x[:, c * LANES:(c + 1) * LANES]


def _load_token_tiles(ref2d, n, nchunk):
    return jnp.concatenate([ref2d[pl.ds(c, n, stride=nchunk), :] for c in range(nchunk)], axis=1)


def _split_bf16(x):
    hi = x.astype(BF16)
    lo = (x - hi.astype(F32)).astype(BF16)
    return hi, lo


def _mod_kernel(c_ref, w_ref, b_ref, o_ref):
    ah, al = _split_bf16(_silu(c_ref[...]))
    wh, wl = _split_bf16(w_ref[0])
    o_ref[0] = _dot(ah, wh) + _dot(al, wh) + _dot(ah, wl) + b_ref[0]


def _modulation(conds, ada_w, ada_b):
    L, D, D6 = ada_w.shape
    RP = conds.shape[0]
    return pl.pallas_call(
        _mod_kernel,
        grid=(L, D6 // D),
        in_specs=[
            pl.BlockSpec((RP, D), lambda l, j: (0, 0)),
            pl.BlockSpec((1, D, D), lambda l, j: (l, 0, j)),
            pl.BlockSpec((1, 1, D), lambda l, j: (l, 0, j)),
        ],
        out_specs=pl.BlockSpec((1, RP, D), lambda l, j: (l, 0, j)),
        out_shape=jax.ShapeDtypeStruct((L, RP, D6), F32),
        compiler_params=_cparams(("parallel", "parallel")),
        name="modulation",
    )(conds, ada_w, ada_b.reshape(L, 1, D6))


def _log_gate(x, la, l1):
    y = l1 + jnp.minimum(x, 0.0) - jnp.log1p(jnp.exp(-jnp.abs(x)))
    return jnp.maximum(la, y) + jnp.log1p(jnp.exp(-jnp.abs(la - y)))


def _stacked_x(xa_ref, xb_ref):
    return jnp.where(pl.program_id(0) == 0, xa_ref[0], xb_ref[0])


def _stacked_x_specs(tm, D, nb):
    return [
        pl.BlockSpec((1, tm, D), lambda r, i: (0, jnp.where(r == 0, i, nb - 1), 0)),
        pl.BlockSpec((1, tm, D), lambda r, i: (jnp.maximum(r - 1, 0), jnp.where(r == 0, 0, i), 0)),
    ]


def _inproj_kernel(xa_ref, xb_ref, g_ref, sc_ref, sh_ref, w_ref, wlo_ref, la_ref, l1_ref,
                   q_ref, k_ref, v_ref, kvf_ref, hp_ref, *, na_w, hg_w):
    u = _rms(_stacked_x(xa_ref, xb_ref)) * g_ref[...] * (1.0 + sc_ref[0]) + sh_ref[0]
    uh, ul = _split_bf16(u)
    p = _dot(uh, w_ref[...])
    g0 = 3 * na_w + hg_w
    gates = p[:, g0:g0 + 2 * hg_w] + _dot(ul, w_ref[:, g0:g0 + 2 * hg_w]) + _dot(uh, wlo_ref[...])
    k = p[:, na_w:2 * na_w]
    v = p[:, 2 * na_w:3 * na_w]
    q_ref[0] = (p[:, :na_w] * (HEAD_DIM ** -0.5)).astype(BF16)
    k_ref[0] = k.astype(BF16)
    v_ref[0] = v.astype(BF16)
    kvf_ref[:, :na_w] = k
    kvf_ref[:, na_w:] = v
    o = 3 * na_w

    def col(j):
        return p[:, o + j * hg_w:o + (j + 1) * hg_w]

    hp_ref[0, :, 0 * hg_w:1 * hg_w] = _silu(col(0))
    hp_ref[0, :, 1 * hg_w:2 * hg_w] = col(3)
    hp_ref[0, :, 2 * hg_w:3 * hg_w] = _log_gate(gates[:, :hg_w], la_ref[0:1, :], l1_ref[0:1, :])
    hp_ref[0, :, 3 * hg_w:4 * hg_w] = _log_gate(gates[:, hg_w:], la_ref[1:2, :], l1_ref[1:2, :])
    hp_ref[0, :, 4 * hg_w:5 * hg_w] = _silu(col(4))
    hp_ref[0, :, 5 * hg_w:6 * hg_w] = col(5)


def _inproj(xa, xb, g, sc, sh, w_in_bf, w_gate_lo, log_lb, log1m_lb):
    _, NT, D = xa.shape
    R = xb.shape[0] + 1
    na_w, hg_w = D // 2, D // 4
    inw = w_in_bf.shape[1]
    tm = TOK_TILE
    nb = NT // tm
    kern = functools.partial(_inproj_kernel, na_w=na_w, hg_w=hg_w)
    tok = lambda r, i: (r, i, 0)
    per_req = lambda r, i: (r, 0, 0)
    const2 = lambda r, i: (0, 0)
    return pl.pallas_call(
        kern,
        grid=(R, nb),
        in_specs=_stacked_x_specs(tm, D, nb) + [
            pl.BlockSpec((1, D), const2),
            pl.BlockSpec((1, 1, D), per_req),
            pl.BlockSpec((1, 1, D), per_req),
            pl.BlockSpec((D, inw), const2),
            pl.BlockSpec((D, 2 * hg_w), const2),
            pl.BlockSpec((2, hg_w), const2),
            pl.BlockSpec((2, hg_w), const2),
        ],
        out_specs=[
            pl.BlockSpec((1, tm, na_w), tok),
            pl.BlockSpec((1, tm, na_w), tok),
            pl.BlockSpec((1, tm, na_w), tok),
            pl.BlockSpec((tm, 2 * na_w), lambda r, i: (jnp.where(r == 0, i, nb), 0)),
            pl.BlockSpec((1, tm, 6 * hg_w), tok),
        ],
        out_shape=[
            jax.ShapeDtypeStruct((R, NT, na_w), BF16),
            jax.ShapeDtypeStruct((R, NT, na_w), BF16),
            jax.ShapeDtypeStruct((R, NT, na_w), BF16),
            jax.ShapeDtypeStruct(((nb + 1) * tm, 2 * na_w), F32),
            jax.ShapeDtypeStruct((R, NT, 6 * hg_w), F32),
        ],
        compiler_params=_cparams(("arbitrary", "arbitrary"), VMEM_LIMIT),
        name="prenorm_inproj",
    )(xa, xb, g, sc, sh, w_in_bf, w_gate_lo, log_lb, log1m_lb)


def _pair_stack(q):
    first = lax.broadcasted_iota(I32, (1, LANES), 1) < HEAD_DIM
    zero = jnp.zeros_like(q)
    return jnp.concatenate([jnp.where(first, q, zero), jnp.where(first, zero, q)], axis=0), first


def _ctx_attn_kernel(q_ref, k_ref, v_ref, o_ref):
    n = q_ref.shape[1]
    q2, first = _pair_stack(q_ref[0])
    s = _dot_nt(q2, k_ref[0])
    p = jnp.exp(s - jnp.max(s, axis=-1, keepdims=True))
    l = jnp.sum(p, axis=-1, keepdims=True)
    o2 = _dot(p.astype(BF16), v_ref[0]) * (1.0 / l)
    o_ref[0] = jnp.where(first, o2[:n], o2[n:]).astype(BF16)


def _ctx_attention(q, k, v, nseq, seq):
    W = q.shape[-1]
    blk = pl.BlockSpec((1, seq, LANES), lambda s, p: (0, s, p))
    return pl.pallas_call(
        _ctx_attn_kernel,
        grid=(nseq, W // LANES),
        in_specs=[blk, blk, blk],
        out_specs=pl.BlockSpec((1, seq, LANES), lambda s, p: (0, s, p)),
        out_shape=jax.ShapeDtypeStruct((1, nseq * seq, W), BF16),
        compiler_params=_cparams(("parallel", "parallel")),
        name="ctx_attention",
    )(q, k, v)


def _na_bias_table(rpb, rows):
    H = rpb.shape[0]
    qr = np.arange(NA_RB)[:, None]
    kr = np.arange(NA_WR)[None, :]
    dr, rvalid = [], []
    for var in range(3):
        delta = (0, -NA_KH // 2, -NA_KH)[var]
        r0 = (0, NA_RB, rows - NA_RB)[var]
        r = r0 + qr
        rk = r0 + delta + kr
        rs = np.clip(r - NA_KH // 2, 0, rows - NA_KH)
        rvalid.append((rk >= rs) & (rk < rs + NA_KH))
        dr.append(np.clip(rk - r + NA_KH - 1, 0, 2 * NA_KH - 2))
    nd = 2 * NA_KH - 1
    slot = np.where(np.stack(rvalid), np.stack(dr), nd)
    onehot = jnp.asarray(np.eye(nd + 1, dtype=np.float32)[slot])
    c = np.arange(GRID_W)[:, None]
    ck = np.arange(GRID_W)[None, :]
    wsc = np.clip(c - NA_KW // 2, 0, GRID_W - NA_KW)
    cvalid = (ck >= wsc) & (ck < wsc + NA_KW)
    dc = np.clip(ck - c + NA_KW - 1, 0, 2 * NA_KW - 2)
    nq, nk = NA_RB * GRID_W, NA_WR * GRID_W
    bias_c = jnp.where(jnp.asarray(cvalid)[None, None], rpb.astype(F32)[:, :, dc], NEG_BIG)
    bias_c = jnp.concatenate([bias_c, jnp.full((H, 1, GRID_W, GRID_W), NEG_BIG, F32)], axis=1)
    tab = jnp.einsum('vqkd,hdcx->vhqckx', onehot, bias_c, precision=lax.Precision.HIGHEST)
    return tab.reshape(3, H // 2, 2 * nq, nk)


def _na_attn_kernel(q_ref, k_ref, v_ref, kc_ref, vc_ref, tab_ref, o_ref, *, rows):
    i = pl.program_id(2)
    nq = NA_RB * GRID_W
    nk = NA_WR * GRID_W
    ws = jnp.clip(NA_RB * i - NA_KH // 2, 0, rows - NA_WR)
    start = pl.multiple_of(ws * GRID_W, 4 * GRID_W)
    q2, first = _pair_stack(q_ref[0])
    kl = k_ref[0, pl.ds(start, nk), :]
    vl = v_ref[0, pl.ds(start, nk), :]
    kc = kc_ref[0, 0].astype(BF16)
    vc = vc_ref[0, 0].astype(BF16)
    s_l = _dot_nt(q2, kl) + tab_ref[0, 0]
    s_c = _dot_nt(q2, kc)
    m = jnp.maximum(jnp.max(s_l, axis=-1, keepdims=True), jnp.max(s_c, axis=-1, keepdims=True))
    p_l = jnp.exp(s_l - m)
    p_c = jnp.exp(s_c - m)
    l = jnp.sum(p_l, axis=-1, keepdims=True) + jnp.sum(p_c, axis=-1, keepdims=True)
    o2 = (_dot(p_l.astype(BF16), vl) + _dot(p_c.astype(BF16), vc)) * (1.0 / l)
    o_ref[0] = jnp.where(first, o2[:nq], o2[nq:]).astype(BF16)


def _na_attention(q, k, v, cache_k, cache_v, layer, tab):
    R, NT, W = q.shape
    nreq = R - 1
    rows = NT // GRID_W
    assert rows % NA_RB == 0 and rows >= NA_WR
    nb = rows // NA_RB
    past = cache_k.shape[2]
    nq = NA_RB * GRID_W
    kern = functools.partial(_na_attn_kernel, rows=rows)
    slab = pl.BlockSpec((1, NT, LANES), lambda b, p, i: (b + 1, 0, p))
    ctx = pl.BlockSpec((1, 1, past, LANES), lambda b, p, i: (b, layer, 0, p))

    def variant(i):
        return jnp.where(i == 0, 0, jnp.where(i == nb - 1, 2, 1))

    return pl.pallas_call(
        kern,
        grid=(nreq, W // LANES, nb),
        in_specs=[
            pl.BlockSpec((1, nq, LANES), lambda b, p, i: (b + 1, i, p)),
            slab, slab, ctx, ctx,
            pl.BlockSpec((1, 1, 2 * nq, NA_WR * GRID_W), lambda b, p, i: (variant(i), p, 0, 0)),
        ],
        out_specs=pl.BlockSpec((1, nq, LANES), lambda b, p, i: (b, i, p)),
        out_shape=jax.ShapeDtypeStruct((nreq, NT, W), BF16),
        compiler_params=_cparams(("parallel", "parallel", "arbitrary"), VMEM_LIMIT),
        name="na_attention",
    )(q, k, v, cache_k, cache_v, tab)


def _hgrn_chunk(q, v, lf, st, rev):
    C, HW = q.shape
    nh = HW // HEAD_DIM
    nsub = C // HG_SUB
    row = lax.broadcasted_iota(I32, (C, 1), 0)
    tau = (C - 1 - row) if rev else row
    b = lf
    d = 1
    while d < C:
        shifted = pltpu.roll(b, (C - d) if rev else d, axis=0)
        b = b + jnp.where(tau >= d, shifted, 0.0)
        d *= 2
    kk = 1.0 - jnp.exp(lf)
    last = 0 if rev else C - 1
    b_last = b[last:last + 1, :]
    q_inter = q * jnp.exp(b)
    k_end = kk * jnp.exp(b_last - b)

    lane_head = lax.broadcasted_iota(I32, (1, HW), 1) >> LOG_HEAD_DIM
    vb = v.astype(BF16)
    a_rows = []
    starts = []
    for i in range(nsub):
        mid_tau = HG_SUB * i + HG_SUB // 2 - 1
        mid = (C - 1 - mid_tau) if rev else mid_tau
        b_mid = b[mid:mid + 1, :]
        ps = (C - HG_SUB * (i + 1)) if rev else HG_SUB * i
        starts.append(ps)
        qs = q[ps:ps + HG_SUB, :] * jnp.exp(jnp.minimum(b[ps:ps + HG_SUB, :] - b_mid, HG_CLAMP))
        ks = (kk * jnp.exp(jnp.minimum(b_mid - b, HG_CLAMP))).astype(BF16)
        lhs = jnp.concatenate([jnp.where(lane_head == h, qs, 0.0) for h in range(nh)], axis=0).astype(BF16)
        a = _dot_nt(lhs, ks)
        t_sub = lax.broadcasted_iota(I32, (nh * HG_SUB, C), 0) & (HG_SUB - 1)
        t_tau = HG_SUB * i + ((HG_SUB - 1 - t_sub) if rev else t_sub)
        s_idx = lax.broadcasted_iota(I32, (nh * HG_SUB, C), 1)
        s_tau = (C - 1 - s_idx) if rev else s_idx
        a_rows.append(jnp.where(s_tau <= t_tau, a, 0.0).astype(BF16))
    av = _dot(jnp.concatenate(a_rows, axis=0), vb)
    pieces = [None] * nsub
    for i in range(nsub):
        acc = jnp.zeros((HG_SUB, HW), F32)
        for h in range(nh):
            r0 = (i * nh + h) * HG_SUB
            acc = acc + jnp.where(lane_head == h, av[r0:r0 + HG_SUB, :], 0.0)
        pieces[starts[i] // HG_SUB] = acc
    o = jnp.concatenate(pieces, axis=0) + _dot_nt(q_inter.astype(BF16), st.astype(BF16))

    rh = lax.broadcasted_iota(I32, (HW, HW), 0) >> LOG_HEAD_DIM
    ch = lax.broadcasted_iota(I32, (HW, HW), 1) >> LOG_HEAD_DIM
    upd = _dot_tn(vb, k_end.astype(BF16))
    st_new = st * jnp.exp(b_last) + jnp.where(rh == ch, upd, 0.0)
    return o, st_new


def _hgrn_kernel(*refs, has_init):
    if has_init:
        qf, vf, lff, qb, vb_, lfb, s0, of, ob, sout, st = refs
    else:
        qf, vf, lff, qb, vb_, lfb, of, ob, sout, st = refs
        s0 = None
    c = pl.program_id(1)

    @pl.when(c == 0)
    def _():
        if has_init:
            st[...] = s0[0]
        else:
            st[...] = jnp.zeros_like(st)

    o, s_new = _hgrn_chunk(qf[0], vf[0], lff[0], st[0], rev=False)
    of[0] = o
    st[0] = s_new
    o, s_new = _hgrn_chunk(qb[0], vb_[0], lfb[0], st[1], rev=True)
    ob[0] = o
    st[1] = s_new

    @pl.when(c == pl.num_programs(1) - 1)
    def _():
        sout[0] = st[...]


def _hgrn(hp, nseq, n, s0=None):
    R, NT, W6 = hp.shape
    HW = W6 // 6
    nc = n // HG_CHUNK
    latent = s0 is not None

    def spec(colblk, back, out=False):
        def index(s, c):
            cc = nc - 1 - c if back else c
            if latent:
                return (s if out else s + 1, cc, colblk)
            return (0, s * nc + cc, colblk)
        return pl.BlockSpec((1, HG_CHUNK, HW), index)

    in_specs = [spec(0, False), spec(1, False), spec(2, False), spec(0, True), spec(1, True), spec(3, True)]
    args = [hp] * 6
    if latent:
        in_specs.append(pl.BlockSpec((1, 2, HW, HW), lambda s, c: (s, 0, 0, 0)))
        args.append(s0)
    nout = nseq if latent else 1
    return pl.pallas_call(
        functools.partial(_hgrn_kernel, has_init=latent),
        grid=(nseq, nc),
        in_specs=in_specs,
        out_specs=[spec(0, False, True), spec(0, True, True),
                   pl.BlockSpec((1, 2, HW, HW), lambda s, c: (s, 0, 0, 0))],
        out_shape=[
            jax.ShapeDtypeStruct((nout, NT, HW), F32),
            jax.ShapeDtypeStruct((nout, NT, HW), F32),
            jax.ShapeDtypeStruct((nseq, 2, HW, HW), F32),
        ],
        scratch_shapes=[pltpu.VMEM((2, HW, HW), F32)],
        compiler_params=_cparams(("parallel", "arbitrary")),
        name="hgrn_scan",
    )(*args)


def _pool_kernel(a_ref, w_ref, sc_ref, o_ref):
    a = a_ref[0]
    n, W = a.shape
    t = lax.broadcasted_iota(I32, (n, 1), 0)

    def shift(x, d):
        src = t - d
        return jnp.where((src >= 0) & (src < n), pltpu.roll(x, d % n, axis=0), 0.0)

    trail = [a]
    fwd = [a]
    for j in range(len(POOL_HALF) - 1):
        h = POOL_HALF[j]
        trail.append(trail[j] + shift(trail[j], h))
        fwd.append(fwd[j] + shift(fwd[j], -h))
    grp = lax.broadcasted_iota(I32, (1, W), 1) >> LOG_HEAD_DIM
    win = jnp.zeros_like(a)
    half = jnp.zeros((1, W), I32)
    for j, h in enumerate(POOL_HALF):
        win = jnp.where(grp == j, shift(trail[j], 1) + fwd[j], win)
        half = jnp.where(grp == j, h, half)
    cnt = jnp.minimum(t + half, n) - jnp.maximum(t - half, 0)
    p = win / cnt.astype(F32) - a
    o_ref[0] = (_dot(p.astype(BF16), w_ref[...]) * sc_ref[...]).astype(BF16)


def _pool(hp, nseq, n, w_bd, scale, latent):
    R, NT, W6 = hp.shape
    W = W6 // 6
    return pl.pallas_call(
        _pool_kernel,
        grid=(nseq,),
        in_specs=[
            pl.BlockSpec((1, n, W), (lambda s: (s + 1, 0, 5)) if latent else (lambda s: (0, s, 5))),
            pl.BlockSpec((W, W), lambda s: (0, 0)),
            pl.BlockSpec((1, W), lambda s: (0, 0)),
        ],
        out_specs=pl.BlockSpec((1, n, W), (lambda s: (s, 0, 0)) if latent else (lambda s: (0, s, 0))),
        out_shape=jax.ShapeDtypeStruct((nseq if latent else 1, NT, W), BF16),
        compiler_params=_cparams(("parallel",), VMEM_LIMIT),
        name="pool_mixer",
    )(hp, w_bd, scale)


def _outproj_kernel(xa_ref, xb_ref, na_a, na_b, of_a, of_b, ob_a, ob_b, op_a, op_b, sg_ref,
                    wo_ref, hgn_ref, seg_ref, g1_ref, gpm_ref, gpf_ref, sc2_ref, sh2_ref, wrh_ref, wrl_ref,
                    x1_ref, u2_ref, lg_ref, *, na_w, hg_w):
    o = _stacked_x(of_a, of_b) + _stacked_x(ob_a, ob_b)
    hi, lo = _split_bf16(o * o)
    ms = _dot(hi, seg_ref[...]) + _dot(lo, seg_ref[...])
    ohg = o * lax.rsqrt(ms + EPS) * hgn_ref[...] * sg_ref[0]
    mix = (_dot(_stacked_x(na_a, na_b), wo_ref[0:na_w, :])
           + _dot(ohg.astype(BF16), wo_ref[na_w:na_w + hg_w, :])
           + _dot(_stacked_x(op_a, op_b), wo_ref[na_w + hg_w:, :]))
    x1 = _stacked_x(xa_ref, xb_ref) + g1_ref[0] * (_rms(mix) * gpm_ref[...])
    x1_ref[0] = x1
    u2 = _rms(x1) * gpf_ref[...] * (1.0 + sc2_ref[0]) + sh2_ref[0]
    _store_token_tiles(u2_ref, u2)
    uh, ul = _split_bf16(u2)
    lg_ref[0] = _dot_nt(wrh_ref[...], uh) + _dot_nt(wrl_ref[...], uh) + _dot_nt(wrh_ref[...], ul)


def _outproj(x, ona, of, ob, opool, hp, wo_bf, hg_norm, seg, g1, gpm, gpf, sc2, sh2, wr_hi, wr_lo):
    _, NT, D = x[0].shape
    R = x[1].shape[0] + 1
    na_w, hg_w = D // 2, D // 4
    E = wr_hi.shape[0]
    tm = TOK_TILE
    nb = NT // tm
    kern = functools.partial(_outproj_kernel, na_w=na_w, hg_w=hg_w)
    tok = lambda r, i: (r, i, 0)
    per_req = lambda r, i: (r, 0, 0)
    const2 = lambda r, i: (0, 0)
    pairs = (_stacked_x_specs(tm, D, nb) + _stacked_x_specs(tm, na_w, nb) + _stacked_x_specs(tm, hg_w, nb)
             + _stacked_x_specs(tm, hg_w, nb) + _stacked_x_specs(tm, hg_w, nb))
    return pl.pallas_call(
        kern,
        grid=(R, nb),
        in_specs=pairs + [
            pl.BlockSpec((1, tm, hg_w), lambda r, i: (r, i, 4)),
            pl.BlockSpec((D, D), const2),
            pl.BlockSpec((1, hg_w), const2),
            pl.BlockSpec((hg_w, hg_w), const2),
            pl.BlockSpec((1, 1, D), per_req),
            pl.BlockSpec((1, D), const2),
            pl.BlockSpec((1, D), const2),
            pl.BlockSpec((1, 1, D), per_req),
            pl.BlockSpec((1, 1, D), per_req),
            pl.BlockSpec((E, D), const2),
            pl.BlockSpec((E, D), const2),
        ],
        out_specs=[
            pl.BlockSpec((1, tm, D), tok),
            pl.BlockSpec((1, tm * (D // LANES), LANES), tok),
            pl.BlockSpec((1, E, tm), lambda r, i: (r, 0, i)),
        ],
        out_shape=[
            jax.ShapeDtypeStruct((R, NT, D), F32),
            jax.ShapeDtypeStruct((R, NT * (D // LANES), LANES), F32),
            jax.ShapeDtypeStruct((R, E, NT), F32),
        ],
        compiler_params=_cparams(("parallel", "parallel"), VMEM_LIMIT),
        name="outproj_router",
    )(*x, *ona, *of, *ob, *opool, hp, wo_bf, hg_norm, seg, g1, gpm, gpf, sc2, sh2, wr_hi, wr_lo)


def _prefix_lanes(mask_f, tri):
    rows, n = mask_f.shape
    T = tri.shape[0]
    outs = []
    carry = jnp.zeros((rows, 1), F32)
    for j in range(n // T):
        seg = mask_f[:, j * T:(j + 1) * T]
        pre = _dot(seg.astype(BF16), tri) + carry
        outs.append(pre)
        carry = pre[:, T - 1:T]
    return outs[0] if len(outs) == 1 else jnp.concatenate(outs, axis=1)


def _router_kernel(lg_ref, tri_ref, idx_ref, aff_ref, rank_sc, *, cap, nrow_groups):
    if nrow_groups == 1:
        lg = lg_ref[0]
        ex = jnp.exp(lg - jnp.max(lg, axis=0, keepdims=True))
        aff = ex / jnp.sum(ex, axis=0, keepdims=True)
    else:
        lg = lg_ref[...]
        ex = jnp.exp(lg - jnp.max(lg, axis=0, keepdims=True))
        aff = ex / jnp.sum(ex, axis=0, keepdims=True)
        aff = aff.reshape(lg.shape[0] * lg.shape[1], lg.shape[2])
    rows, n = aff.shape
    aff_ref[...] = aff.reshape(aff_ref.shape)
    thr = jnp.zeros((rows, 1), I32)
    for bit in range(30, -1, -1):
        cand = thr | (1 << bit)
        cnt = jnp.sum((aff >= pltpu.bitcast(cand, F32)).astype(F32), axis=-1, keepdims=True)
        thr = jnp.where(cnt >= cap, cand, thr)
    thr_f = pltpu.bitcast(thr, F32)
    gt = aff > thr_f
    eq = aff == thr_f
    need = cap - jnp.sum(gt.astype(F32), axis=-1, keepdims=True)
    tri = tri_ref[...]
    eq_f = eq.astype(F32)
    eq_before = _prefix_lanes(eq_f, tri) - eq_f
    sel = gt | (eq & (eq_before < need))
    rank_sc[...] = _prefix_lanes(sel.astype(F32), tri)

    ncol = idx_ref.shape[-1]
    jcol = lax.broadcasted_iota(I32, (cap, 1), 0).astype(F32)
    lane = lax.broadcasted_iota(I32, (1, ncol), 1)

    def body(r, acc):
        rk = rank_sc[pl.ds(r, 1), :]
        pos = jnp.sum((rk <= jcol).astype(F32), axis=-1, keepdims=True)
        return jnp.where(lane == r, pos, acc)

    idx = lax.fori_loop(0, rows, body, jnp.zeros((cap, ncol), F32))
    idx_ref[...] = idx.astype(I32).reshape(idx_ref.shape)


def _router_seg_kernel(lg_ref, tri_ref, mexc_ref, minc_ref, idx_ref, aff_ref, loc_sc, *, cap):
    lg = lg_ref[0]
    E, S, _ = lg.shape
    rows = E * S
    log_s = S.bit_length() - 1
    ex = jnp.exp(lg - jnp.max(lg, axis=0, keepdims=True))
    aff = ex / jnp.sum(ex, axis=0, keepdims=True)
    aff_ref[0] = aff
    thr = jnp.zeros((E, 1, 1), I32)
    for bit in range(30, -1, -1):
        cand = thr | (1 << bit)
        cnt = jnp.sum((aff >= pltpu.bitcast(cand, F32)).astype(F32), axis=(1, 2), keepdims=True)
        thr = jnp.where(cnt >= cap, cand, thr)
    thr_f = pltpu.bitcast(thr, F32)
    gt = aff > thr_f
    eq = aff == thr_f
    need = cap - jnp.sum(gt.astype(F32), axis=(1, 2), keepdims=True)
    tri = tri_ref[...]
    eq2 = eq.astype(F32).reshape(rows, LANES)
    eq_loc = _dot(eq2.astype(BF16), tri)
    eq_tot = jnp.broadcast_to(eq_loc[:, LANES - 1:LANES], (rows, LANES)).astype(BF16)
    eq_before = (eq_loc - eq2 + _dot(mexc_ref[...], eq_tot)).reshape(E, S, LANES)
    sel = (gt | (eq & (eq_before < need))).astype(BF16).reshape(rows, LANES)
    loc_sc[...] = _dot(sel, tri).astype(BF16)
    tot_l = _dot_nt(jnp.ones((8, LANES), BF16), sel)
    pinc_l = _dot(tot_l.astype(BF16), minc_ref[...])[0:1]
    pexc_l = pinc_l - tot_l[0:1]
    jcol = lax.broadcasted_iota(I32, (cap, 1), 0).astype(F32)
    lane_e = lax.broadcasted_iota(I32, (1, rows), 1) >> log_s
    lane = lax.broadcasted_iota(I32, (1, idx_ref.shape[-1]), 1)

    def body(e, acc):
        mine = lane_e == e
        inside = mine & (pexc_l <= jcol) & (jcol < pinc_l)
        nfull = jnp.sum((mine & (pinc_l <= jcol)).astype(F32), axis=-1, keepdims=True)
        before = jnp.sum(jnp.where(inside, pexc_l, 0.0), axis=-1, keepdims=True)
        rsel = _dot(inside.astype(BF16), loc_sc[...])
        cnt = jnp.sum((rsel <= jcol - before).astype(F32), axis=-1, keepdims=True)
        return jnp.where(lane == e, LANES * nfull + cnt, acc)

    idx = lax.fori_loop(0, E, body, jnp.zeros((cap, idx_ref.shape[-1]), F32))
    idx_ref[0] = idx.astype(I32)


def _router_seg(logits, cap):
    G, E, N = logits.shape
    S = N // LANES
    assert S & (S - 1) == 0
    rows = E * S
    tri = jnp.asarray(np.triu(np.ones((LANES, LANES), np.float32)), BF16)
    mexc = jnp.asarray(np.kron(np.eye(E), np.tril(np.ones((S, S)), -1)), BF16)
    minc = jnp.asarray(np.kron(np.eye(E), np.triu(np.ones((S, S)))), BF16)
    idx, aff = pl.pallas_call(
        functools.partial(_router_seg_kernel, cap=cap),
        grid=(G,),
        in_specs=[
            pl.BlockSpec((1, E, S, LANES), lambda g: (g, 0, 0, 0)),
            pl.BlockSpec((LANES, LANES), lambda g: (0, 0)),
            pl.BlockSpec((rows, rows), lambda g: (0, 0)),
            pl.BlockSpec((rows, rows), lambda g: (0, 0)),
        ],
        out_specs=[pl.BlockSpec((1, cap, LANES), lambda g: (g, 0, 0)),
                   pl.BlockSpec((1, E, S, LANES), lambda g: (g, 0, 0, 0))],
        out_shape=[jax.ShapeDtypeStruct((G, cap, LANES), I32), jax.ShapeDtypeStruct((G, E, S, LANES), F32)],
        scratch_shapes=[pltpu.VMEM((rows, LANES), BF16)],
        compiler_params=_cparams(("parallel",), VMEM_LIMIT),
        name="router_topk_seg",
    )(logits.reshape(G, E, S, LANES), tri, mexc, minc)
    return idx, aff.reshape(G, E, N)


def _router(logits, nq, n, cap, per_step_requests):
    G, E, NTOT = logits.shape
    rows = E * nq
    ncol = max(LANES, rows)
    T = min(n, 2 * LANES)
    tri = jnp.asarray(np.triu(np.ones((T, T), np.float32)), BF16)
    kern = functools.partial(_router_kernel, cap=cap, nrow_groups=nq)
    if nq == 1:
        lg_spec = pl.BlockSpec((1, E, n), lambda g: (g, 0, 0))
        lg_in = logits
        aff_spec = pl.BlockSpec((1, E, n), lambda g: (g, 0, 0))
        aff_shape = jax.ShapeDtypeStruct((G, E, n), F32)
    else:
        assert G == 1
        lg_in = logits.reshape(E, nq, n)
        lg_spec = pl.BlockSpec((E, nq, n), lambda g: (0, 0, 0))
        aff_spec = pl.BlockSpec((rows, n), lambda g: (0, 0))
        aff_shape = jax.ShapeDtypeStruct((rows, n), F32)
    return pl.pallas_call(
        kern,
        grid=(G,),
        in_specs=[lg_spec, pl.BlockSpec((T, T), lambda g: (0, 0))],
        out_specs=[pl.BlockSpec((1, cap, ncol), lambda g: (g, 0, 0)), aff_spec],
        out_shape=[jax.ShapeDtypeStruct((G, cap, ncol), I32), aff_shape],
        scratch_shapes=[pltpu.VMEM((rows, n), F32)],
        compiler_params=_cparams(("parallel",), VMEM_LIMIT),
        name="router_topk",
    )(lg_in, tri)


def _ffn_kernel(idx_ref, aff_ref, u_ref, wg_ref, wu_ref, wd_ref, y_hbm, xs, outs, y_acc, sem, *, cap, nchunk):
    r = pl.program_id(0)
    e = pl.program_id(1)
    unroll = 8

    def tile(i):
        return pl.ds(pl.multiple_of(i * nchunk, nchunk), nchunk)

    @pl.when(e == 0)
    def _():
        y_acc[...] = jnp.zeros_like(y_acc)

    def gather(jb, carry):
        for u in range(unroll):
            j = jb * unroll + u
            xs[tile(j), :] = u_ref[0, tile(idx_ref[0, 0, 0, j]), :]
        return carry

    lax.fori_loop(0, cap // unroll, gather, 0)
    xb = _load_token_tiles(xs, cap, nchunk).astype(BF16)
    hid = _silu(_dot(xb, wg_ref[0])) * _dot(xb, wu_ref[0])
    o = _dot(hid.astype(BF16), wd_ref[0])
    for c in range(nchunk):
        outs[pl.ds(c, cap, stride=nchunk), :] = o[:, c * LANES:(c + 1) * LANES]

    def scatter(jb, carry):
        js = [jb * unroll + u for u in range(unroll)]
        ts = [idx_ref[0, 0, 0, j] for j in js]
        new = [y_acc[tile(t), :] + outs[tile(j), :] * aff_ref[0, 0, 0, t] for j, t in zip(js, ts)]
        for t, val in zip(ts, new):
            y_acc[tile(t), :] = val
        return carry

    lax.fori_loop(0, cap // unroll, scatter, 0)

    @pl.when(e == pl.num_programs(1) - 1)
    def _():
        cp = pltpu.make_async_copy(y_acc, y_hbm.at[r], sem)
        cp.start()
        cp.wait()


def _expert_ffn(u2, idx, aff, wg, wu, wd):
    E, D, FF = wg.shape
    nchunk = D // LANES
    R, rows, _ = u2.shape
    NT = rows // nchunk
    cap = idx.shape[-1]
    return pl.pallas_call(
        functools.partial(_ffn_kernel, cap=cap, nchunk=nchunk),
        grid=(R, E),
        in_specs=[
            pl.BlockSpec((1, 1, 1, cap), lambda r, e: (r, e, 0, 0), memory_space=pltpu.SMEM),
            pl.BlockSpec((1, 1, 1, NT), lambda r, e: (r, e, 0, 0), memory_space=pltpu.SMEM),
            pl.BlockSpec((1, rows, LANES), lambda r, e: (r, 0, 0), pipeline_mode=pl.Buffered(1)),
            pl.BlockSpec((1, D, FF), lambda r, e: (e, 0, 0)),
            pl.BlockSpec((1, D, FF), lambda r, e: (e, 0, 0)),
            pl.BlockSpec((1, FF, D), lambda r, e: (e, 0, 0)),
        ],
        out_specs=pl.BlockSpec(memory_space=pl.ANY),
        out_shape=jax.ShapeDtypeStruct((R, rows, LANES), F32),
        scratch_shapes=[
            pltpu.VMEM((cap * nchunk, LANES), F32),
            pltpu.VMEM((cap * nchunk, LANES), F32),
            pltpu.VMEM((rows, LANES), F32),
            pltpu.SemaphoreType.DMA(()),
        ],
        compiler_params=_cparams(("arbitrary", "arbitrary"), 60 * 1024 * 1024),
        name="expert_ffn",
    )(idx.reshape(R, E, 1, cap), aff.reshape(R, E, 1, NT), u2, wg, wu, wd)


def _resid_kernel(x_ref, ff_ref, g2_ref, gn_ref, o_ref):
    n, d = x_ref.shape[1:]
    ff = _load_token_tiles(ff_ref.at[0], n, d // LANES)
    o_ref[0] = x_ref[0] + g2_ref[0] * (_rms(ff) * gn_ref[...])


def _ffn_residual(x1, ff, g2, gn, r0, nr):
    _, NT, D = x1.shape
    tm = TOK_TILE
    tok = lambda r, i: (r0 + r, i, 0)
    return pl.pallas_call(
        _resid_kernel,
        grid=(nr, NT // tm),
        in_specs=[
            pl.BlockSpec((1, tm, D), tok),
            pl.BlockSpec((1, tm * (D // LANES), LANES), tok),
            pl.BlockSpec((1, 1, D), lambda r, i: (r0 + r, 0, 0)),
            pl.BlockSpec((1, D), lambda r, i: (0, 0)),
        ],
        out_specs=pl.BlockSpec((1, tm, D), lambda r, i: (r, i, 0)),
        out_shape=jax.ShapeDtypeStruct((nr, NT, D), F32),
        compiler_params=_cparams(("parallel", "parallel")),
        name="ffn_residual",
    )(x1, ff, g2, gn)


def kernel(x_prompt, x_sample, cache_na_k, cache_na_v, state_hgrn, c, c_ctx, ada_w, ada_b,
           norm_pre_mix, norm_post_mix, norm_pre_ffn, norm_post_ffn, w_in, w_out, na_rpb,
           hg_lb, hg_norm, pool_w, pool_scale, w_router, w_gate, w_up, w_down):
    B, SEQ, D = x_prompt.shape
    NB, NT, _ = x_sample.shape
    L = ada_w.shape[0]
    assert B * SEQ == NT, "context tokens are stacked as one request of the latent sequence length"
    R = NB + 1
    na_w, hg_w = D // 2, D // 4
    H = na_w // HEAD_DIM
    nh = hg_w // HEAD_DIM
    E = w_router.shape[-1]
    past = cache_na_k.shape[2]
    cap_ctx = 2 * SEQ // E
    cap_lat = 2 * NT // E
    assert B * cap_ctx == cap_lat

    xa, xb = x_prompt.reshape(1, NT, D), x_sample
    RP = 16
    conds = jnp.zeros((RP, D), F32).at[0].set(c_ctx).at[1:R].set(c)
    mod = _modulation(conds, ada_w, ada_b)[:, :R].reshape(L, R, 6, 1, D)

    lb = jnp.cumsum(jax.nn.softmax(hg_lb.astype(F32), axis=0), axis=0)
    lb = lb - lb[:1]
    log_lb, log1m_lb = jnp.log(lb), jnp.log1p(-lb)

    ck = cache_na_k.reshape(NB, L, past, na_w)
    cv = cache_na_v.reshape(NB, L, past, na_w)
    seg = jnp.asarray(np.kron(np.eye(nh), np.full((HEAD_DIM, HEAD_DIM), 1.0 / HEAD_DIM)), BF16)
    eye_h = jnp.eye(nh, dtype=F32)
    s0_all = jnp.einsum('bldhkv,hg->bldhvgk', state_hgrn.astype(F32), eye_h).reshape(NB, L, 2, hg_w, hg_w)

    new_k, new_v, new_s = [], [], []
    for l in range(L):
        sh1, sc1, g1, sh2, sc2, g2 = (mod[l, :, j] for j in range(6))
        w_bf = w_in[l].astype(BF16)
        g0 = 3 * na_w + hg_w
        w_gate_lo = (w_in[l][:, g0:g0 + 2 * hg_w] - w_bf[:, g0:g0 + 2 * hg_w].astype(F32)).astype(BF16)
        q, k, v, kvf, hp = _inproj(xa, xb, norm_pre_mix[l][None], sc1, sh1, w_bf, w_gate_lo,
                                   log_lb[l], log1m_lb[l])
        new_k.append(kvf[:NT, :na_w].reshape(B, SEQ, H, HEAD_DIM))
        new_v.append(kvf[:NT, na_w:].reshape(B, SEQ, H, HEAD_DIM))

        tab = _na_bias_table(na_rpb[l], NT // GRID_W)
        ona = (_ctx_attention(q, k, v, B, SEQ), _na_attention(q, k, v, ck, cv, l, tab))

        of_c, ob_c, st_c = _hgrn(hp, B, SEQ)
        of_l, ob_l, _ = _hgrn(hp, NB, NT, s0=s0_all[:, l])
        st_c = st_c.reshape(B, 2, nh, HEAD_DIM, nh, HEAD_DIM)
        new_s.append(jnp.einsum('bdhvgk,hg->bdhkv', st_c, eye_h))

        w_bd = jnp.einsum('gcd,gh->gchd', pool_w[l].astype(F32), jnp.eye(len(POOL_HALF), dtype=F32))
        w_bd = w_bd.reshape(hg_w, hg_w).astype(BF16)
        opool = (_pool(hp, B, SEQ, w_bd, pool_scale[l][None], False),
                 _pool(hp, NB, NT, w_bd, pool_scale[l][None], True))

        wr_hi, wr_lo = _split_bf16(w_router[l].T.astype(F32))
        x1, u2, logits = _outproj((xa, xb), ona, (of_c, of_l), (ob_c, ob_l), opool, hp, w_out[l].astype(BF16),
                                  hg_norm[l][None], seg, g1, norm_post_mix[l][None], norm_pre_ffn[l][None],
                                  sc2, sh2, wr_hi, wr_lo)

        idx_c, aff_c = _router(logits[:1], B, SEQ, cap_ctx, False)
        idx_l, aff_l = _router_seg(logits[1:], cap_lat)
        idx_c = idx_c[0, :, :E * B].reshape(cap_ctx, E, B).transpose(1, 2, 0)
        idx_c = (idx_c + (jnp.arange(B, dtype=I32) * SEQ)[None, :, None]).reshape(1, E, cap_lat)
        idx = jnp.concatenate([idx_c, idx_l[:, :, :E].transpose(0, 2, 1)], axis=0)
        aff = jnp.concatenate([aff_c.reshape(1, E, NT), aff_l], axis=0)

        ff = _expert_ffn(u2, idx, aff, w_gate[l].astype(BF16), w_up[l].astype(BF16), w_down[l].astype(BF16))
        xa = _ffn_residual(x1, ff, g2, norm_post_ffn[l][None], 0, 1)
        xb = _ffn_residual(x1, ff, g2, norm_post_ffn[l][None], 1, NB)

    y_prompt = xa.reshape(B, SEQ, D)
    y_sample = xb
    new_cache_k = jnp.stack(new_k, axis=1)
    new_cache_v = jnp.stack(new_v, axis=1)
    new_state = jnp.stack(new_s, axis=1).astype(x_prompt.dtype)
    return (y_prompt, y_sample, new_cache_k, new_cache_v, new_state)
```

```python
import functools

import numpy as np
import jax
import jax.numpy as jnp
from jax import lax
from jax.experimental import pallas as pl
from jax.experimental.pallas import tpu as pltpu

F32 = jnp.float32
BF16 = jnp.bfloat16
I32 = jnp.int32

HEAD_DIM = 64
LOG_HEAD_DIM = 6
LANES = 128
GRID_W = 64
NA_KH = 8
NA_KW = 16
NA_RB = 8
NA_WR = 16
HG_CHUNK = 64
HG_SUB = 16
HG_CLAMP = 80.0
POOL_HALF = (1, 2, 4, 8)
N_EXPERTS = 16
EPS = 1e-6
NEG_BIG = -1e30
TOK_TILE = 512
VMEM_LIMIT = 56 * 1024 * 1024


def _cparams(sem, vmem=None):
    return pltpu.CompilerParams(dimension_semantics=sem, vmem_limit_bytes=vmem)


def _silu(x):
    return x / (1.0 + jnp.exp(-x))


def _dot(a, b):
    return jnp.dot(a, b, preferred_element_type=F32)


def _dot_nt(a, b):
    return lax.dot_general(a, b, (((1,), (1,)), ((), ())), preferred_element_type=F32)


def _dot_tn(a, b):
    return lax.dot_general(a, b, (((0,), (0,)), ((), ())), preferred_element_type=F32)


def _rms(x):
    return x * lax.rsqrt(jnp.mean(x * x, axis=-1, keepdims=True) + EPS)


def _store_token_tiles(ref, x):
    n, d = x.shape
    nchunk = d // LANES
    for c in range(nchunk):
        ref[0, pl.ds(c, n, stride=nchunk), :] = x[:, c * LANES:(c + 1) * LANES]


def _load_token_tiles(ref2d, n, nchunk):
    return jnp.concatenate([ref2d[pl.ds(c, n, stride=nchunk), :] for c in range(nchunk)], axis=1)


def _split_bf16(x):
    hi = x.astype(BF16)
    lo = (x - hi.astype(F32)).astype(BF16)
    return hi, lo


def _mod_kernel(c_ref, w_ref, b_ref, o_ref):
    ah, al = _split_bf16(_silu(c_ref[...]))
    wh, wl = _split_bf16(w_ref[0])
    o_ref[0] = _dot(ah, wh) + _dot(al, wh) + _dot(ah, wl) + b_ref[0]


def _modulation(conds, ada_w, ada_b):
    L, D, D6 = ada_w.shape
    RP = conds.shape[0]
    return pl.pallas_call(
        _mod_kernel,
        grid=(L, D6 // D),
        in_specs=[
            pl.BlockSpec((RP, D), lambda l, j: (0, 0)),
            pl.BlockSpec((1, D, D), lambda l, j: (l, 0, j)),
            pl.BlockSpec((1, 1, D), lambda l, j: (l, 0, j)),
        ],
        out_specs=pl.BlockSpec((1, RP, D), lambda l, j: (l, 0, j)),
        out_shape=jax.ShapeDtypeStruct((L, RP, D6), F32),
        compiler_params=_cparams(("parallel", "parallel")),
        name="modulation",
    )(conds, ada_w, ada_b.reshape(L, 1, D6))


def _log_gate(x, la, l1):
    y = l1 + jnp.minimum(x, 0.0) - jnp.log1p(jnp.exp(-jnp.abs(x)))
    return jnp.maximum(la, y) + jnp.log1p(jnp.exp(-jnp.abs(la - y)))


def _stacked_x(xa_ref, xb_ref):
    return jnp.where(pl.program_id(0) == 0, xa_ref[0], xb_ref[0])


def _stacked_x_specs(tm, D, nb):
    return [
        pl.BlockSpec((1, tm, D), lambda r, i: (0, jnp.where(r == 0, i, nb - 1), 0)),
        pl.BlockSpec((1, tm, D), lambda r, i: (jnp.maximum(r - 1, 0), jnp.where(r == 0, 0, i), 0)),
    ]


def _inproj_kernel(xa_ref, xb_ref, g_ref, sc_ref, sh_ref, w_ref, wlo_ref, la_ref, l1_ref,
                   q_ref, k_ref, v_ref, kvf_ref, hp_ref, *, na_w, hg_w):
    u = _rms(_stacked_x(xa_ref, xb_ref)) * g_ref[...] * (1.0 + sc_ref[0]) + sh_ref[0]
    uh, ul = _split_bf16(u)
    p = _dot(uh, w_ref[...])
    g0 = 3 * na_w + hg_w
    gates = p[:, g0:g0 + 2 * hg_w] + _dot(ul, w_ref[:, g0:g0 + 2 * hg_w]) + _dot(uh, wlo_ref[...])
    k = p[:, na_w:2 * na_w]
    v = p[:, 2 * na_w:3 * na_w]
    q_ref[0] = (p[:, :na_w] * (HEAD_DIM ** -0.5)).astype(BF16)
    k_ref[0] = k.astype(BF16)
    v_ref[0] = v.astype(BF16)
    kvf_ref[:, :na_w] = k
    kvf_ref[:, na_w:] = v
    o = 3 * na_w

    def col(j):
        return p[:, o + j * hg_w:o + (j + 1) * hg_w]

    hp_ref[0, :, 0 * hg_w:1 * hg_w] = _silu(col(0))
    hp_ref[0, :, 1 * hg_w:2 * hg_w] = col(3)
    hp_ref[0, :, 2 * hg_w:3 * hg_w] = _log_gate(gates[:, :hg_w], la_ref[0:1, :], l1_ref[0:1, :])
    hp_ref[0, :, 3 * hg_w:4 * hg_w] = _log_gate(gates[:, hg_w:], la_ref[1:2, :], l1_ref[1:2, :])
    hp_ref[0, :, 4 * hg_w:5 * hg_w] = _silu(col(4))
    hp_ref[0, :, 5 * hg_w:6 * hg_w] = col(5)


def _inproj(xa, xb, g, sc, sh, w_in_bf, w_gate_lo, log_lb, log1m_lb):
    _, NT, D = xa.shape
    R = xb.shape[0] + 1
    na_w, hg_w = D // 2, D // 4
    inw = w_in_bf.shape[1]
    tm = TOK_TILE
    nb = NT // tm
    kern = functools.partial(_inproj_kernel, na_w=na_w, hg_w=hg_w)
    tok = lambda r, i: (r, i, 0)
    per_req = lambda r, i: (r, 0, 0)
    const2 = lambda r, i: (0, 0)
    return pl.pallas_call(
        kern,
        grid=(R, nb),
        in_specs=_stacked_x_specs(tm, D, nb) + [
            pl.BlockSpec((1, D), const2),
            pl.BlockSpec((1, 1, D), per_req),
            pl.BlockSpec((1, 1, D), per_req),
            pl.BlockSpec((D, inw), const2),
            pl.BlockSpec((D, 2 * hg_w), const2),
            pl.BlockSpec((2, hg_w), const2),
            pl.BlockSpec((2, hg_w), const2),
        ],
        out_specs=[
            pl.BlockSpec((1, tm, na_w), tok),
            pl.BlockSpec((1, tm, na_w), tok),
            pl.BlockSpec((1, tm, na_w), tok),
            pl.BlockSpec((tm, 2 * na_w), lambda r, i: (jnp.where(r == 0, i, nb), 0)),
            pl.BlockSpec((1, tm, 6 * hg_w), tok),
        ],
        out_shape=[
            jax.ShapeDtypeStruct((R, NT, na_w), BF16),
            jax.ShapeDtypeStruct((R, NT, na_w), BF16),
            jax.ShapeDtypeStruct((R, NT, na_w), BF16),
            jax.ShapeDtypeStruct(((nb + 1) * tm, 2 * na_w), F32),
            jax.ShapeDtypeStruct((R, NT, 6 * hg_w), F32),
        ],
        compiler_params=_cparams(("arbitrary", "arbitrary"), VMEM_LIMIT),
        name="prenorm_inproj",
    )(xa, xb, g, sc, sh, w_in_bf, w_gate_lo, log_lb, log1m_lb)


def _pair_stack(q):
    first = lax.broadcasted_iota(I32, (1, LANES), 1) < HEAD_DIM
    zero = jnp.zeros_like(q)
    return jnp.concatenate([jnp.where(first, q, zero), jnp.where(first, zero, q)], axis=0), first


def _ctx_attn_kernel(q_ref, k_ref, v_ref, o_ref):
    n = q_ref.shape[1]
    q2, first = _pair_stack(q_ref[0])
    s = _dot_nt(q2, k_ref[0])
    p = jnp.exp(s - jnp.max(s, axis=-1, keepdims=True))
    l = jnp.sum(p, axis=-1, keepdims=True)
    o2 = _dot(p.astype(BF16), v_ref[0]) * (1.0 / l)
    o_ref[0] = jnp.where(first, o2[:n], o2[n:]).astype(BF16)


def _ctx_attention(q, k, v, nseq, seq):
    W = q.shape[-1]
    blk = pl.BlockSpec((1, seq, LANES), lambda s, p: (0, s, p))
    return pl.pallas_call(
        _ctx_attn_kernel,
        grid=(nseq, W // LANES),
        in_specs=[blk, blk, blk],
        out_specs=pl.BlockSpec((1, seq, LANES), lambda s, p: (0, s, p)),
        out_shape=jax.ShapeDtypeStruct((1, nseq * seq, W), BF16),
        compiler_params=_cparams(("parallel", "parallel")),
        name="ctx_attention",
    )(q, k, v)


def _na_bias_table(rpb):
    H = rpb.shape[0]
    c = np.arange(GRID_W)[:, None]
    ck = np.arange(GRID_W)[None, :]
    wsc = np.clip(c - NA_KW // 2, 0, GRID_W - NA_KW)
    cvalid = (ck >= wsc) & (ck < wsc + NA_KW)
    dc = np.clip(ck - c + NA_KW - 1, 0, 2 * NA_KW - 2)
    bias_c = jnp.where(jnp.asarray(cvalid)[None, None], rpb.astype(F32)[:, :, dc], NEG_BIG)
    strips = [bias_c[:, d:d + NA_KH].transpose(0, 2, 1, 3).reshape(H, GRID_W, NA_KH * GRID_W) for d in range(NA_KH)]
    return jnp.stack(strips, axis=1)


def _na_attn_kernel(q_ref, k_ref, v_ref, kc_ref, vc_ref, ts_ref, o_ref, pc_sc, *, rows):
    i = pl.program_id(2)
    nq = NA_RB * GRID_W
    nkl = NA_KH * GRID_W
    q2, first = _pair_stack(q_ref[0])
    kc = kc_ref[0, 0].astype(BF16)
    vc = vc_ref[0, 0].astype(BF16)
    s_c = _dot_nt(q2, kc)
    o_loc, l_inv = [], []
    for qr in range(NA_RB):
        r = NA_RB * i + qr
        rs = jnp.clip(r - NA_KH // 2, 0, rows - NA_KH)
        d0 = rs - r + NA_KH - 1
        start = pl.multiple_of(rs * GRID_W, GRID_W)
        a, b = qr * GRID_W, nq + qr * GRID_W
        q2r = jnp.concatenate([q2[a:a + GRID_W], q2[b:b + GRID_W]], axis=0)
        bias = jnp.concatenate([ts_ref[0, d0], ts_ref[1, d0]], axis=0)
        s_l = _dot_nt(q2r, k_ref[0, pl.ds(start, nkl), :]) + bias
        s_cr = jnp.concatenate([s_c[a:a + GRID_W], s_c[b:b + GRID_W]], axis=0)
        m = jnp.maximum(jnp.max(s_l, axis=-1, keepdims=True), jnp.max(s_cr, axis=-1, keepdims=True))
        p_l = jnp.exp(s_l - m)
        p_c = jnp.exp(s_cr - m)
        l_inv.append(1.0 / (jnp.sum(p_l, axis=-1, keepdims=True) + jnp.sum(p_c, axis=-1, keepdims=True)))
        o_loc.append(_dot(p_l.astype(BF16), v_ref[0, pl.ds(start, nkl), :]))
        pc_sc[2 * a:2 * a + 2 * GRID_W, :] = p_c.astype(BF16)
    o_ctx = _dot(pc_sc[...], vc)
    for qr in range(NA_RB):
        a = qr * GRID_W
        o2 = (o_loc[qr] + o_ctx[2 * a:2 * a + 2 * GRID_W]) * l_inv[qr]
        o_ref[0, a:a + GRID_W, :] = jnp.where(first, o2[:GRID_W], o2[GRID_W:]).astype(BF16)


def _na_attention(q, k, v, cache_k, cache_v, layer, tab):
    R, NT, W = q.shape
    nreq = R - 1
    rows = NT // GRID_W
    assert rows % NA_RB == 0 and rows >= NA_KH
    nb = rows // NA_RB
    past = cache_k.shape[2]
    nq = NA_RB * GRID_W
    kern = functools.partial(_na_attn_kernel, rows=rows)
    slab = pl.BlockSpec((1, NT, LANES), lambda b, p, i: (b + 1, 0, p))
    ctx = pl.BlockSpec((1, 1, past, LANES), lambda b, p, i: (b, layer, 0, p))
    return pl.pallas_call(
        kern,
        grid=(nreq, W // LANES, nb),
        in_specs=[
            pl.BlockSpec((1, nq, LANES), lambda b, p, i: (b + 1, i, p)),
            slab, slab, ctx, ctx,
            pl.BlockSpec((2, NA_KH, GRID_W, NA_KH * GRID_W), lambda b, p, i: (p, 0, 0, 0)),
        ],
        out_specs=pl.BlockSpec((1, nq, LANES), lambda b, p, i: (b, i, p)),
        out_shape=jax.ShapeDtypeStruct((nreq, NT, W), BF16),
        scratch_shapes=[pltpu.VMEM((2 * nq, past), BF16)],
        compiler_params=_cparams(("parallel", "parallel", "arbitrary"), VMEM_LIMIT),
        name="na_attention",
    )(q, k, v, cache_k, cache_v, tab)


def _hgrn_chunk(q, v, lf, st, rev):
    C, HW = q.shape
    nh = HW // HEAD_DIM
    nsub = C // HG_SUB
    row = lax.broadcasted_iota(I32, (C, 1), 0)
    tau = (C - 1 - row) if rev else row
    b = lf
    d = 1
    while d < C:
        shifted = pltpu.roll(b, (C - d) if rev else d, axis=0)
        b = b + jnp.where(tau >= d, shifted, 0.0)
        d *= 2
    kk = 1.0 - jnp.exp(lf)
    last = 0 if rev else C - 1
    b_last = b[last:last + 1, :]
    q_inter = q * jnp.exp(b)
    k_end = kk * jnp.exp(b_last - b)

    lane_head = lax.broadcasted_iota(I32, (1, HW), 1) >> LOG_HEAD_DIM
    vb = v.astype(BF16)
    a_rows = []
    starts = []
    for i in range(nsub):
        mid_tau = HG_SUB * i + HG_SUB // 2 - 1
        mid = (C - 1 - mid_tau) if rev else mid_tau
        b_mid = b[mid:mid + 1, :]
        ps = (C - HG_SUB * (i + 1)) if rev else HG_SUB * i
        starts.append(ps)
        qs = q[ps:ps + HG_SUB, :] * jnp.exp(jnp.minimum(b[ps:ps + HG_SUB, :] - b_mid, HG_CLAMP))
        ks = (kk * jnp.exp(jnp.minimum(b_mid - b, HG_CLAMP))).astype(BF16)
        lhs = jnp.concatenate([jnp.where(lane_head == h, qs, 0.0) for h in range(nh)], axis=0).astype(BF16)
        a = _dot_nt(lhs, ks)
        t_sub = lax.broadcasted_iota(I32, (nh * HG_SUB, C), 0) & (HG_SUB - 1)
        t_tau = HG_SUB * i + ((HG_SUB - 1 - t_sub) if rev else t_sub)
        s_idx = lax.broadcasted_iota(I32, (nh * HG_SUB, C), 1)
        s_tau = (C - 1 - s_idx) if rev else s_idx
        a_rows.append(jnp.where(s_tau <= t_tau, a, 0.0).astype(BF16))
    av = _dot(jnp.concatenate(a_rows, axis=0), vb)
    pieces = [None] * nsub
    for i in range(nsub):
        acc = jnp.zeros((HG_SUB, HW), F32)
        for h in range(nh):
            r0 = (i * nh + h) * HG_SUB
            acc = acc + jnp.where(lane_head == h, av[r0:r0 + HG_SUB, :], 0.0)
        pieces[starts[i] // HG_SUB] = acc
    o = jnp.concatenate(pieces, axis=0) + _dot_nt(q_inter.astype(BF16), st.astype(BF16))

    rh = lax.broadcasted_iota(I32, (HW, HW), 0) >> LOG_HEAD_DIM
    ch = lax.broadcasted_iota(I32, (HW, HW), 1) >> LOG_HEAD_DIM
    upd = _dot_tn(vb, k_end.astype(BF16))
    st_new = st * jnp.exp(b_last) + jnp.where(rh == ch, upd, 0.0)
    return o, st_new


def _hgrn_kernel(*refs, has_init):
    if has_init:
        qf, vf, lff, qb, vb_, lfb, s0, of, ob, sout, st = refs
    else:
        qf, vf, lff, qb, vb_, lfb, of, ob, sout, st = refs
        s0 = None
    c = pl.program_id(1)

    @pl.when(c == 0)
    def _():
        if has_init:
            st[...] = s0[0]
        else:
            st[...] = jnp.zeros_like(st)

    o, s_new = _hgrn_chunk(qf[0], vf[0], lff[0], st[0], rev=False)
    of[0] = o
    st[0] = s_new
    o, s_new = _hgrn_chunk(qb[0], vb_[0], lfb[0], st[1], rev=True)
    ob[0] = o
    st[1] = s_new

    @pl.when(c == pl.num_programs(1) - 1)
    def _():
        sout[0] = st[...]


def _hgrn(hp, nseq, n, s0=None):
    R, NT, W6 = hp.shape
    HW = W6 // 6
    nc = n // HG_CHUNK
    latent = s0 is not None

    def spec(colblk, back, out=False):
        def index(s, c):
            cc = nc - 1 - c if back else c
            if latent:
                return (s if out else s + 1, cc, colblk)
            return (0, s * nc + cc, colblk)
        return pl.BlockSpec((1, HG_CHUNK, HW), index)

    in_specs = [spec(0, False), spec(1, False), spec(2, False), spec(0, True), spec(1, True), spec(3, True)]
    args = [hp] * 6
    if latent:
        in_specs.append(pl.BlockSpec((1, 2, HW, HW), lambda s, c: (s, 0, 0, 0)))
        args.append(s0)
    nout = nseq if latent else 1
    return pl.pallas_call(
        functools.partial(_hgrn_kernel, has_init=latent),
        grid=(nseq, nc),
        in_specs=in_specs,
        out_specs=[spec(0, False, True), spec(0, True, True),
                   pl.BlockSpec((1, 2, HW, HW), lambda s, c: (s, 0, 0, 0))],
        out_shape=[
            jax.ShapeDtypeStruct((nout, NT, HW), F32),
            jax.ShapeDtypeStruct((nout, NT, HW), F32),
            jax.ShapeDtypeStruct((nseq, 2, HW, HW), F32),
        ],
        scratch_shapes=[pltpu.VMEM((2, HW, HW), F32)],
        compiler_params=_cparams(("parallel", "arbitrary")),
        name="hgrn_scan",
    )(*args)


def _pool_kernel(a_ref, w_ref, sc_ref, o_ref):
    a = a_ref[0]
    n, W = a.shape
    t = lax.broadcasted_iota(I32, (n, 1), 0)

    def shift(x, d):
        src = t - d
        return jnp.where((src >= 0) & (src < n), pltpu.roll(x, d % n, axis=0), 0.0)

    trail = [a]
    fwd = [a]
    for j in range(len(POOL_HALF) - 1):
        h = POOL_HALF[j]
        trail.append(trail[j] + shift(trail[j], h))
        fwd.append(fwd[j] + shift(fwd[j], -h))
    grp = lax.broadcasted_iota(I32, (1, W), 1) >> LOG_HEAD_DIM
    win = jnp.zeros_like(a)
    half = jnp.zeros((1, W), I32)
    for j, h in enumerate(POOL_HALF):
        win = jnp.where(grp == j, shift(trail[j], 1) + fwd[j], win)
        half = jnp.where(grp == j, h, half)
    cnt = jnp.minimum(t + half, n) - jnp.maximum(t - half, 0)
    p = win / cnt.astype(F32) - a
    o_ref[0] = (_dot(p.astype(BF16), w_ref[...]) * sc_ref[...]).astype(BF16)


def _pool(hp, nseq, n, w_bd, scale, latent):
    R, NT, W6 = hp.shape
    W = W6 // 6
    return pl.pallas_call(
        _pool_kernel,
        grid=(nseq,),
        in_specs=[
            pl.BlockSpec((1, n, W), (lambda s: (s + 1, 0, 5)) if latent else (lambda s: (0, s, 5))),
            pl.BlockSpec((W, W), lambda s: (0, 0)),
            pl.BlockSpec((1, W), lambda s: (0, 0)),
        ],
        out_specs=pl.BlockSpec((1, n, W), (lambda s: (s, 0, 0)) if latent else (lambda s: (0, s, 0))),
        out_shape=jax.ShapeDtypeStruct((nseq if latent else 1, NT, W), BF16),
        compiler_params=_cparams(("parallel",), VMEM_LIMIT),
        name="pool_mixer",
    )(hp, w_bd, scale)


def _outproj_kernel(xa_ref, xb_ref, na_a, na_b, of_a, of_b, ob_a, ob_b, op_a, op_b, sg_ref,
                    wo_ref, hgn_ref, seg_ref, g1_ref, gpm_ref, gpf_ref, sc2_ref, sh2_ref, wrh_ref, wrl_ref,
                    x1_ref, u2_ref, lg_ref, *, na_w, hg_w):
    o = _stacked_x(of_a, of_b) + _stacked_x(ob_a, ob_b)
    hi, lo = _split_bf16(o * o)
    ms = _dot(hi, seg_ref[...]) + _dot(lo, seg_ref[...])
    ohg = o * lax.rsqrt(ms + EPS) * hgn_ref[...] * sg_ref[0]
    mix = (_dot(_stacked_x(na_a, na_b), wo_ref[0:na_w, :])
           + _dot(ohg.astype(BF16), wo_ref[na_w:na_w + hg_w, :])
           + _dot(_stacked_x(op_a, op_b), wo_ref[na_w + hg_w:, :]))
    x1 = _stacked_x(xa_ref, xb_ref) + g1_ref[0] * (_rms(mix) * gpm_ref[...])
    x1_ref[0] = x1
    u2 = _rms(x1) * gpf_ref[...] * (1.0 + sc2_ref[0]) + sh2_ref[0]
    _store_token_tiles(u2_ref, u2)
    uh, ul = _split_bf16(u2)
    lg_ref[0] = _dot_nt(wrh_ref[...], uh) + _dot_nt(wrl_ref[...], uh) + _dot_nt(wrh_ref[...], ul)


def _outproj(x, ona, of, ob, opool, hp, wo_bf, hg_norm, seg, g1, gpm, gpf, sc2, sh2, wr_hi, wr_lo):
    _, NT, D = x[0].shape
    R = x[1].shape[0] + 1
    na_w, hg_w = D // 2, D // 4
    E = wr_hi.shape[0]
    tm = TOK_TILE
    nb = NT // tm
    kern = functools.partial(_outproj_kernel, na_w=na_w, hg_w=hg_w)
    tok = lambda r, i: (r, i, 0)
    per_req = lambda r, i: (r, 0, 0)
    const2 = lambda r, i: (0, 0)
    pairs = (_stacked_x_specs(tm, D, nb) + _stacked_x_specs(tm, na_w, nb) + _stacked_x_specs(tm, hg_w, nb)
             + _stacked_x_specs(tm, hg_w, nb) + _stacked_x_specs(tm, hg_w, nb))
    return pl.pallas_call(
        kern,
        grid=(R, nb),
        in_specs=pairs + [
            pl.BlockSpec((1, tm, hg_w), lambda r, i: (r, i, 4)),
            pl.BlockSpec((D, D), const2),
            pl.BlockSpec((1, hg_w), const2),
            pl.BlockSpec((hg_w, hg_w), const2),
            pl.BlockSpec((1, 1, D), per_req),
            pl.BlockSpec((1, D), const2),
            pl.BlockSpec((1, D), const2),
            pl.BlockSpec((1, 1, D), per_req),
            pl.BlockSpec((1, 1, D), per_req),
            pl.BlockSpec((E, D), const2),
            pl.BlockSpec((E, D), const2),
        ],
        out_specs=[
            pl.BlockSpec((1, tm, D), tok),
            pl.BlockSpec((1, tm * (D // LANES), LANES), tok),
            pl.BlockSpec((1, E, tm), lambda r, i: (r, 0, i)),
        ],
        out_shape=[
            jax.ShapeDtypeStruct((R, NT, D), F32),
            jax.ShapeDtypeStruct((R, NT * (D // LANES), LANES), F32),
            jax.ShapeDtypeStruct((R, E, NT), F32),
        ],
        compiler_params=_cparams(("parallel", "parallel"), VMEM_LIMIT),
        name="outproj_router",
    )(*x, *ona, *of, *ob, *opool, hp, wo_bf, hg_norm, seg, g1, gpm, gpf, sc2, sh2, wr_hi, wr_lo)


def _prefix_lanes(mask_f, tri):
    rows, n = mask_f.shape
    T = tri.shape[0]
    outs = []
    carry = jnp.zeros((rows, 1), F32)
    for j in range(n // T):
        seg = mask_f[:, j * T:(j + 1) * T]
        pre = _dot(seg.astype(BF16), tri) + carry
        outs.append(pre)
        carry = pre[:, T - 1:T]
    return outs[0] if len(outs) == 1 else jnp.concatenate(outs, axis=1)


def _router_kernel(lg_ref, tri_ref, idx_ref, aff_ref, rank_sc, *, cap, nrow_groups):
    if nrow_groups == 1:
        lg = lg_ref[0]
        ex = jnp.exp(lg - jnp.max(lg, axis=0, keepdims=True))
        aff = ex / jnp.sum(ex, axis=0, keepdims=True)
    else:
        lg = lg_ref[...]
        ex = jnp.exp(lg - jnp.max(lg, axis=0, keepdims=True))
        aff = ex / jnp.sum(ex, axis=0, keepdims=True)
        aff = aff.reshape(lg.shape[0] * lg.shape[1], lg.shape[2])
    rows, n = aff.shape
    aff_ref[...] = aff.reshape(aff_ref.shape)
    thr = jnp.zeros((rows, 1), I32)
    for bit in range(30, -1, -1):
        cand = thr | (1 << bit)
        cnt = jnp.sum((aff >= pltpu.bitcast(cand, F32)).astype(F32), axis=-1, keepdims=True)
        thr = jnp.where(cnt >= cap, cand, thr)
    thr_f = pltpu.bitcast(thr, F32)
    gt = aff > thr_f
    eq = aff == thr_f
    need = cap - jnp.sum(gt.astype(F32), axis=-1, keepdims=True)
    tri = tri_ref[...]
    eq_f = eq.astype(F32)
    eq_before = _prefix_lanes(eq_f, tri) - eq_f
    sel = gt | (eq & (eq_before < need))
    rank_sc[...] = _prefix_lanes(sel.astype(F32), tri)

    ncol = idx_ref.shape[-1]
    jcol = lax.broadcasted_iota(I32, (cap, 1), 0).astype(F32)
    lane = lax.broadcasted_iota(I32, (1, ncol), 1)

    def body(r, acc):
        rk = rank_sc[pl.ds(r, 1), :]
        pos = jnp.sum((rk <= jcol).astype(F32), axis=-1, keepdims=True)
        return jnp.where(lane == r, pos, acc)

    idx = lax.fori_loop(0, rows, body, jnp.zeros((cap, ncol), F32))
    idx_ref[...] = idx.astype(I32).reshape(idx_ref.shape)


def _router_seg_kernel(lg_ref, tri_ref, mexc_ref, minc_ref, idx_ref, aff_ref, loc_sc, *, cap):
    lg = lg_ref[0]
    E, S, _ = lg.shape
    rows = E * S
    log_s = S.bit_length() - 1
    ex = jnp.exp(lg - jnp.max(lg, axis=0, keepdims=True))
    aff = ex / jnp.sum(ex, axis=0, keepdims=True)
    aff_ref[0] = aff
    thr = jnp.zeros((E, 1, 1), I32)
    for bit in range(30, -1, -1):
        cand = thr | (1 << bit)
        cnt = jnp.sum((aff >= pltpu.bitcast(cand, F32)).astype(F32), axis=(1, 2), keepdims=True)
        thr = jnp.where(cnt >= cap, cand, thr)
    thr_f = pltpu.bitcast(thr, F32)
    gt = aff > thr_f
    eq = aff == thr_f
    need = cap - jnp.sum(gt.astype(F32), axis=(1, 2), keepdims=True)
    tri = tri_ref[...]
    eq2 = eq.astype(F32).reshape(rows, LANES)
    eq_loc = _dot(eq2.astype(BF16), tri)
    eq_tot = jnp.broadcast_to(eq_loc[:, LANES - 1:LANES], (rows, LANES)).astype(BF16)
    eq_before = (eq_loc - eq2 + _dot(mexc_ref[...], eq_tot)).reshape(E, S, LANES)
    sel = (gt | (eq & (eq_before < need))).astype(BF16).reshape(rows, LANES)
    loc_sc[...] = _dot(sel, tri).astype(BF16)
    tot_l = _dot_nt(jnp.ones((8, LANES), BF16), sel)
    pinc_l = _dot(tot_l.astype(BF16), minc_ref[...])[0:1]
    pexc_l = pinc_l - tot_l[0:1]
    jcol = lax.broadcasted_iota(I32, (cap, 1), 0).astype(F32)
    lane_e = lax.broadcasted_iota(I32, (1, rows), 1) >> log_s
    lane = lax.broadcasted_iota(I32, (1, idx_ref.shape[-1]), 1)

    def body(e, acc):
        mine = lane_e == e
        inside = mine & (pexc_l <= jcol) & (jcol < pinc_l)
        nfull = jnp.sum((mine & (pinc_l <= jcol)).astype(F32), axis=-1, keepdims=True)
        before = jnp.sum(jnp.where(inside, pexc_l, 0.0), axis=-1, keepdims=True)
        rsel = _dot(inside.astype(BF16), loc_sc[...])
        cnt = jnp.sum((rsel <= jcol - before).astype(F32), axis=-1, keepdims=True)
        return jnp.where(lane == e, LANES * nfull + cnt, acc)

    idx = lax.fori_loop(0, E, body, jnp.zeros((cap, idx_ref.shape[-1]), F32))
    idx_ref[0] = idx.astype(I32)


def _router_seg(logits, cap):
    G, E, N = logits.shape
    S = N // LANES
    assert S & (S - 1) == 0
    rows = E * S
    tri = jnp.asarray(np.triu(np.ones((LANES, LANES), np.float32)), BF16)
    mexc = jnp.asarray(np.kron(np.eye(E), np.tril(np.ones((S, S)), -1)), BF16)
    minc = jnp.asarray(np.kron(np.eye(E), np.triu(np.ones((S, S)))), BF16)
    idx, aff = pl.pallas_call(
        functools.partial(_router_seg_kernel, cap=cap),
        grid=(G,),
        in_specs=[
            pl.BlockSpec((1, E, S, LANES), lambda g: (g, 0, 0, 0)),
            pl.BlockSpec((LANES, LANES), lambda g: (0, 0)),
            pl.BlockSpec((rows, rows), lambda g: (0, 0)),
            pl.BlockSpec((rows, rows), lambda g: (0, 0)),
        ],
        out_specs=[pl.BlockSpec((1, cap, LANES), lambda g: (g, 0, 0)),
                   pl.BlockSpec((1, E, S, LANES), lambda g: (g, 0, 0, 0))],
        out_shape=[jax.ShapeDtypeStruct((G, cap, LANES), I32), jax.ShapeDtypeStruct((G, E, S, LANES), F32)],
        scratch_shapes=[pltpu.VMEM((rows, LANES), BF16)],
        compiler_params=_cparams(("parallel",), VMEM_LIMIT),
        name="router_topk_seg",
    )(logits.reshape(G, E, S, LANES), tri, mexc, minc)
    return idx, aff.reshape(G, E, N)


def _router(logits, nq, n, cap, per_step_requests):
    G, E, NTOT = logits.shape
    rows = E * nq
    ncol = max(LANES, rows)
    T = min(n, 2 * LANES)
    tri = jnp.asarray(np.triu(np.ones((T, T), np.float32)), BF16)
    kern = functools.partial(_router_kernel, cap=cap, nrow_groups=nq)
    if nq == 1:
        lg_spec = pl.BlockSpec((1, E, n), lambda g: (g, 0, 0))
        lg_in = logits
        aff_spec = pl.BlockSpec((1, E, n), lambda g: (g, 0, 0))
        aff_shape = jax.ShapeDtypeStruct((G, E, n), F32)
    else:
        assert G == 1
        lg_in = logits.reshape(E, nq, n)
        lg_spec = pl.BlockSpec((E, nq, n), lambda g: (0, 0, 0))
        aff_spec = pl.BlockSpec((rows, n), lambda g: (0, 0))
        aff_shape = jax.ShapeDtypeStruct((rows, n), F32)
    return pl.pallas_call(
        kern,
        grid=(G,),
        in_specs=[lg_spec, pl.BlockSpec((T, T), lambda g: (0, 0))],
        out_specs=[pl.BlockSpec((1, cap, ncol), lambda g: (g, 0, 0)), aff_spec],
        out_shape=[jax.ShapeDtypeStruct((G, cap, ncol), I32), aff_shape],
        scratch_shapes=[pltpu.VMEM((rows, n), F32)],
        compiler_params=_cparams(("parallel",), VMEM_LIMIT),
        name="router_topk",
    )(lg_in, tri)


def _ffn_kernel(idx_ref, aff_ref, u_ref, wg_ref, wu_ref, wd_ref, y_hbm, xs, outs, y_acc, sem, *, cap, nchunk):
    r = pl.program_id(0)
    e = pl.program_id(1)
    unroll = 8

    def tile(i):
        return pl.ds(pl.multiple_of(i * nchunk, nchunk), nchunk)

    @pl.when(e == 0)
    def _():
        y_acc[...] = jnp.zeros_like(y_acc)

    def gather(jb, carry):
        for u in range(unroll):
            j = jb * unroll + u
            xs[tile(j), :] = u_ref[0, tile(idx_ref[0, 0, 0, j]), :]
        return carry

    lax.fori_loop(0, cap // unroll, gather, 0)
    xb = _load_token_tiles(xs, cap, nchunk).astype(BF16)
    hid = _silu(_dot(xb, wg_ref[0, 0])) * _dot(xb, wu_ref[0, 0])
    o = _dot(hid.astype(BF16), wd_ref[0, 0])
    for c in range(nchunk):
        outs[pl.ds(c, cap, stride=nchunk), :] = o[:, c * LANES:(c + 1) * LANES]

    def scatter(jb, carry):
        js = [jb * unroll + u for u in range(unroll)]
        ts = [idx_ref[0, 0, 0, j] for j in js]
        new = [y_acc[tile(t), :] + outs[tile(j), :] * aff_ref[0, 0, 0, t] for j, t in zip(js, ts)]
        for t, val in zip(ts, new):
            y_acc[tile(t), :] = val
        return carry

    lax.fori_loop(0, cap // unroll, scatter, 0)

    @pl.when(e == pl.num_programs(1) - 1)
    def _():
        cp = pltpu.make_async_copy(y_acc, y_hbm.at[r], sem)
        cp.start()
        cp.wait()


def _expert_ffn(u2, idx, aff, wg, wu, wd, layer):
    _, E, D, FF = wg.shape
    nchunk = D // LANES
    R, rows, _ = u2.shape
    NT = rows // nchunk
    cap = idx.shape[-1]
    return pl.pallas_call(
        functools.partial(_ffn_kernel, cap=cap, nchunk=nchunk),
        grid=(R, E),
        in_specs=[
            pl.BlockSpec((1, 1, 1, cap), lambda r, e: (r, e, 0, 0), memory_space=pltpu.SMEM),
            pl.BlockSpec((1, 1, 1, NT), lambda r, e: (r, e, 0, 0), memory_space=pltpu.SMEM),
            pl.BlockSpec((1, rows, LANES), lambda r, e: (r, 0, 0), pipeline_mode=pl.Buffered(1)),
            pl.BlockSpec((1, 1, D, FF), lambda r, e: (layer, e, 0, 0)),
            pl.BlockSpec((1, 1, D, FF), lambda r, e: (layer, e, 0, 0)),
            pl.BlockSpec((1, 1, FF, D), lambda r, e: (layer, e, 0, 0)),
        ],
        out_specs=pl.BlockSpec(memory_space=pl.ANY),
        out_shape=jax.ShapeDtypeStruct((R, rows, LANES), F32),
        scratch_shapes=[
            pltpu.VMEM((cap * nchunk, LANES), F32),
            pltpu.VMEM((cap * nchunk, LANES), F32),
            pltpu.VMEM((rows, LANES), F32),
            pltpu.SemaphoreType.DMA(()),
        ],
        compiler_params=_cparams(("arbitrary", "arbitrary"), 60 * 1024 * 1024),
        name="expert_ffn",
    )(idx.reshape(R, E, 1, cap), aff.reshape(R, E, 1, NT), u2, wg, wu, wd)


def _resid_kernel(x_ref, ff_ref, g2_ref, gn_ref, o_ref):
    n, d = x_ref.shape[1:]
    ff = _load_token_tiles(ff_ref.at[0], n, d // LANES)
    o_ref[0] = x_ref[0] + g2_ref[0] * (_rms(ff) * gn_ref[...])


def _ffn_residual(x1, ff, g2, gn, r0, nr):
    _, NT, D = x1.shape
    tm = TOK_TILE
    tok = lambda r, i: (r0 + r, i, 0)
    return pl.pallas_call(
        _resid_kernel,
        grid=(nr, NT // tm),
        in_specs=[
            pl.BlockSpec((1, tm, D), tok),
            pl.BlockSpec((1, tm * (D // LANES), LANES), tok),
            pl.BlockSpec((1, 1, D), lambda r, i: (r0 + r, 0, 0)),
            pl.BlockSpec((1, D), lambda r, i: (0, 0)),
        ],
        out_specs=pl.BlockSpec((1, tm, D), lambda r, i: (r, i, 0)),
        out_shape=jax.ShapeDtypeStruct((nr, NT, D), F32),
        compiler_params=_cparams(("parallel", "parallel")),
        name="ffn_residual",
    )(x1, ff, g2, gn)


def kernel(x_prompt, x_sample, cache_na_k, cache_na_v, state_hgrn, c, c_ctx, ada_w, ada_b,
           norm_pre_mix, norm_post_mix, norm_pre_ffn, norm_post_ffn, w_in, w_out, na_rpb,
           hg_lb, hg_norm, pool_w, pool_scale, w_router, w_gate, w_up, w_down):
    B, SEQ, D = x_prompt.shape
    NB, NT, _ = x_sample.shape
    L = ada_w.shape[0]
    assert B * SEQ == NT, "context tokens are stacked as one request of the latent sequence length"
    R = NB + 1
    na_w, hg_w = D // 2, D // 4
    H = na_w // HEAD_DIM
    nh = hg_w // HEAD_DIM
    E = w_router.shape[-1]
    past = cache_na_k.shape[2]
    cap_ctx = 2 * SEQ // E
    cap_lat = 2 * NT // E
    assert B * cap_ctx == cap_lat

    xa, xb = x_prompt.reshape(1, NT, D), x_sample
    RP = 16
    conds = jnp.zeros((RP, D), F32).at[0].set(c_ctx).at[1:R].set(c)
    mod = _modulation(conds, ada_w, ada_b)[:, :R].reshape(L, R, 6, 1, D)

    lb = jnp.cumsum(jax.nn.softmax(hg_lb.astype(F32), axis=0), axis=0)
    lb = lb - lb[:1]
    log_lb, log1m_lb = jnp.log(lb), jnp.log1p(-lb)

    ck = cache_na_k.reshape(NB, L, past, na_w)
    cv = cache_na_v.reshape(NB, L, past, na_w)
    seg = jnp.asarray(np.kron(np.eye(nh), np.full((HEAD_DIM, HEAD_DIM), 1.0 / HEAD_DIM)), BF16)
    eye_h = jnp.eye(nh, dtype=F32)
    s0_all = jnp.einsum('bldhkv,hg->bldhvgk', state_hgrn.astype(F32), eye_h).reshape(NB, L, 2, hg_w, hg_w)

    wg_bf, wu_bf, wd_bf = w_gate.astype(BF16), w_up.astype(BF16), w_down.astype(BF16)
    new_k, new_v, new_s = [], [], []
    for l in range(L):
        sh1, sc1, g1, sh2, sc2, g2 = (mod[l, :, j] for j in range(6))
        w_bf = w_in[l].astype(BF16)
        g0 = 3 * na_w + hg_w
        w_gate_lo = (w_in[l][:, g0:g0 + 2 * hg_w] - w_bf[:, g0:g0 + 2 * hg_w].astype(F32)).astype(BF16)
        q, k, v, kvf, hp = _inproj(xa, xb, norm_pre_mix[l][None], sc1, sh1, w_bf, w_gate_lo,
                                   log_lb[l], log1m_lb[l])
        new_k.append(kvf[:NT, :na_w].reshape(B, SEQ, H, HEAD_DIM))
        new_v.append(kvf[:NT, na_w:].reshape(B, SEQ, H, HEAD_DIM))

        tab = _na_bias_table(na_rpb[l])
        ona = (_ctx_attention(q, k, v, B, SEQ), _na_attention(q, k, v, ck, cv, l, tab))

        of_c, ob_c, st_c = _hgrn(hp, B, SEQ)
        of_l, ob_l, _ = _hgrn(hp, NB, NT, s0=s0_all[:, l])
        st_c = st_c.reshape(B, 2, nh, HEAD_DIM, nh, HEAD_DIM)
        new_s.append(jnp.einsum('bdhvgk,hg->bdhkv', st_c, eye_h))

        w_bd = jnp.einsum('gcd,gh->gchd', pool_w[l].astype(F32), jnp.eye(len(POOL_HALF), dtype=F32))
        w_bd = w_bd.reshape(hg_w, hg_w).astype(BF16)
        opool = (_pool(hp, B, SEQ, w_bd, pool_scale[l][None], False),
                 _pool(hp, NB, NT, w_bd, pool_scale[l][None], True))

        wr_hi, wr_lo = _split_bf16(w_router[l].T.astype(F32))
        x1, u2, logits = _outproj((xa, xb), ona, (of_c, of_l), (ob_c, ob_l), opool, hp, w_out[l].astype(BF16),
                                  hg_norm[l][None], seg, g1, norm_post_mix[l][None], norm_pre_ffn[l][None],
                                  sc2, sh2, wr_hi, wr_lo)

        idx_c, aff_c = _router(logits[:1], B, SEQ, cap_ctx, False)
        idx_l, aff_l = _router_seg(logits[1:], cap_lat)
        idx_c = idx_c[0, :, :E * B].reshape(cap_ctx, E, B).transpose(1, 2, 0)
        idx_c = (idx_c + (jnp.arange(B, dtype=I32) * SEQ)[None, :, None]).reshape(1, E, cap_lat)
        idx = jnp.concatenate([idx_c, idx_l[:, :, :E].transpose(0, 2, 1)], axis=0)
        aff = jnp.concatenate([aff_c.reshape(1, E, NT), aff_l], axis=0)

        ff = _expert_ffn(u2, idx, aff, wg_bf, wu_bf, wd_bf, l)
        xa = _ffn_residual(x1, ff, g2, norm_post_ffn[l][None], 0, 1)
        xb = _ffn_residual(x1, ff, g2, norm_post_ffn[l][None], 1, NB)

    y_prompt = xa.reshape(B, SEQ, D)
    y_sample = xb
    new_cache_k = jnp.stack(new_k, axis=1)
    new_cache_v = jnp.stack(new_v, axis=1)
    new_state = jnp.stack(new_s, axis=1).astype(x_prompt.dtype)
    return (y_prompt, y_sample, new_cache_k, new_cache_v, new_state)
```

```python
import functools

import numpy as np
import jax
import jax.numpy as jnp
from jax import lax
from jax.experimental import pallas as pl
from jax.experimental.pallas import tpu as pltpu

F32 = jnp.float32
BF16 = jnp.bfloat16
I32 = jnp.int32

HEAD_DIM = 64
LOG_HEAD_DIM = 6
LANES = 128
GRID_W = 64
NA_KH = 8
NA_KW = 16
NA_RB = 8
NA_WR = 16
HG_CHUNK = 64
HG_SUB = 16
HG_CLAMP = 80.0
POOL_HALF = (1, 2, 4, 8)
N_EXPERTS = 16
EPS = 1e-6
NEG_BIG = -1e30
TOK_TILE = 512
VMEM_LIMIT = 56 * 1024 * 1024


def _cparams(sem, vmem=None):
    return pltpu.CompilerParams(dimension_semantics=sem, vmem_limit_bytes=vmem)


def _silu(x):
    return x / (1.0 + jnp.exp(-x))


def _dot(a, b):
    return jnp.dot(a, b, preferred_element_type=F32)


def _dot_nt(a, b):
    return lax.dot_general(a, b, (((1,), (1,)), ((), ())), preferred_element_type=F32)


def _dot_tn(a, b):
    return lax.dot_general(a, b, (((0,), (0,)), ((), ())), preferred_element_type=F32)


def _rms(x):
    return x * lax.rsqrt(jnp.mean(x * x, axis=-1, keepdims=True) + EPS)


def _store_token_tiles(ref, x):
    n, d = x.shape
    nchunk = d // LANES
    for c in range(nchunk):
        ref[0, pl.ds(c, n, stride=nchunk), :] = x[:, c * LANES:(c + 1) * LANES]


def _load_token_tiles(ref2d, n, nchunk):
    return jnp.concatenate([ref2d[pl.ds(c, n, stride=nchunk), :] for c in range(nchunk)], axis=1)


def _split_bf16(x):
    hi = x.astype(BF16)
    lo = (x - hi.astype(F32)).astype(BF16)
    return hi, lo


def _mod_kernel(c_ref, w_ref, b_ref, o_ref):
    ah, al = _split_bf16(_silu(c_ref[...]))
    wh, wl = _split_bf16(w_ref[0])
    o_ref[0] = _dot(ah, wh) + _dot(al, wh) + _dot(ah, wl) + b_ref[0]


def _modulation(conds, ada_w, ada_b):
    L, D, D6 = ada_w.shape
    RP = conds.shape[0]
    return pl.pallas_call(
        _mod_kernel,
        grid=(L, D6 // D),
        in_specs=[
            pl.BlockSpec((RP, D), lambda l, j: (0, 0)),
            pl.BlockSpec((1, D, D), lambda l, j: (l, 0, j)),
            pl.BlockSpec((1, 1, D), lambda l, j: (l, 0, j)),
        ],
        out_specs=pl.BlockSpec((1, RP, D), lambda l, j: (l, 0, j)),
        out_shape=jax.ShapeDtypeStruct((L, RP, D6), F32),
        compiler_params=_cparams(("parallel", "parallel")),
        name="modulation",
    )(conds, ada_w, ada_b.reshape(L, 1, D6))


def _log_gate(x, la, l1):
    y = l1 + jnp.minimum(x, 0.0) - jnp.log1p(jnp.exp(-jnp.abs(x)))
    return jnp.maximum(la, y) + jnp.log1p(jnp.exp(-jnp.abs(la - y)))


def _stacked_x(xa_ref, xb_ref):
    return jnp.where(pl.program_id(0) == 0, xa_ref[0], xb_ref[0])


def _stacked_x_specs(tm, D, nb):
    return [
        pl.BlockSpec((1, tm, D), lambda r, i: (0, jnp.where(r == 0, i, nb - 1), 0)),
        pl.BlockSpec((1, tm, D), lambda r, i: (jnp.maximum(r - 1, 0), jnp.where(r == 0, 0, i), 0)),
    ]


def _inproj_kernel(xa_ref, xb_ref, g_ref, sc_ref, sh_ref, w_ref, wlo_ref, la_ref, l1_ref,
                   q_ref, k_ref, v_ref, kvf_ref, hp_ref, *, na_w, hg_w):
    u = _rms(_stacked_x(xa_ref, xb_ref)) * g_ref[...] * (1.0 + sc_ref[0]) + sh_ref[0]
    uh, ul = _split_bf16(u)
    p = _dot(uh, w_ref[...])
    g0 = 3 * na_w + hg_w
    gates = p[:, g0:g0 + 2 * hg_w] + _dot(ul, w_ref[:, g0:g0 + 2 * hg_w]) + _dot(uh, wlo_ref[...])
    k = p[:, na_w:2 * na_w]
    v = p[:, 2 * na_w:3 * na_w]
    q_ref[0] = (p[:, :na_w] * (HEAD_DIM ** -0.5)).astype(BF16)
    k_ref[0] = k.astype(BF16)
    v_ref[0] = v.astype(BF16)
    kvf_ref[:, :na_w] = k
    kvf_ref[:, na_w:] = v
    o = 3 * na_w

    def col(j):
        return p[:, o + j * hg_w:o + (j + 1) * hg_w]

    hp_ref[0, :, 0 * hg_w:1 * hg_w] = _silu(col(0))
    hp_ref[0, :, 1 * hg_w:2 * hg_w] = col(3)
    hp_ref[0, :, 2 * hg_w:3 * hg_w] = _log_gate(gates[:, :hg_w], la_ref[0:1, :], l1_ref[0:1, :])
    hp_ref[0, :, 3 * hg_w:4 * hg_w] = _log_gate(gates[:, hg_w:], la_ref[1:2, :], l1_ref[1:2, :])
    hp_ref[0, :, 4 * hg_w:5 * hg_w] = _silu(col(4))
    hp_ref[0, :, 5 * hg_w:6 * hg_w] = col(5)


def _inproj(xa, xb, g, sc, sh, w_in_bf, w_gate_lo, log_lb, log1m_lb):
    _, NT, D = xa.shape
    R = xb.shape[0] + 1
    na_w, hg_w = D // 2, D // 4
    inw = w_in_bf.shape[1]
    tm = TOK_TILE
    nb = NT // tm
    kern = functools.partial(_inproj_kernel, na_w=na_w, hg_w=hg_w)
    tok = lambda r, i: (r, i, 0)
    per_req = lambda r, i: (r, 0, 0)
    const2 = lambda r, i: (0, 0)
    return pl.pallas_call(
        kern,
        grid=(R, nb),
        in_specs=_stacked_x_specs(tm, D, nb) + [
            pl.BlockSpec((1, D), const2),
            pl.BlockSpec((1, 1, D), per_req),
            pl.BlockSpec((1, 1, D), per_req),
            pl.BlockSpec((D, inw), const2),
            pl.BlockSpec((D, 2 * hg_w), const2),
            pl.BlockSpec((2, hg_w), const2),
            pl.BlockSpec((2, hg_w), const2),
        ],
        out_specs=[
            pl.BlockSpec((1, tm, na_w), tok),
            pl.BlockSpec((1, tm, na_w), tok),
            pl.BlockSpec((1, tm, na_w), tok),
            pl.BlockSpec((tm, 2 * na_w), lambda r, i: (jnp.where(r == 0, i, nb), 0)),
            pl.BlockSpec((1, tm, 6 * hg_w), tok),
        ],
        out_shape=[
            jax.ShapeDtypeStruct((R, NT, na_w), BF16),
            jax.ShapeDtypeStruct((R, NT, na_w), BF16),
            jax.ShapeDtypeStruct((R, NT, na_w), BF16),
            jax.ShapeDtypeStruct(((nb + 1) * tm, 2 * na_w), F32),
            jax.ShapeDtypeStruct((R, NT, 6 * hg_w), F32),
        ],
        compiler_params=_cparams(("arbitrary", "arbitrary"), VMEM_LIMIT),
        name="prenorm_inproj",
    )(xa, xb, g, sc, sh, w_in_bf, w_gate_lo, log_lb, log1m_lb)


def _pair_stack(q):
    first = lax.broadcasted_iota(I32, (1, LANES), 1) < HEAD_DIM
    zero = jnp.zeros_like(q)
    return jnp.concatenate([jnp.where(first, q, zero), jnp.where(first, zero, q)], axis=0), first


def _ctx_attn_kernel(q_ref, k_ref, v_ref, o_ref):
    n = q_ref.shape[1]
    q2, first = _pair_stack(q_ref[0])
    s = _dot_nt(q2, k_ref[0])
    p = jnp.exp(s - jnp.max(s, axis=-1, keepdims=True))
    l = jnp.sum(p, axis=-1, keepdims=True)
    o2 = _dot(p.astype(BF16), v_ref[0]) * (1.0 / l)
    o_ref[0] = jnp.where(first, o2[:n], o2[n:]).astype(BF16)


def _ctx_attention(q, k, v, nseq, seq):
    W = q.shape[-1]
    blk = pl.BlockSpec((1, seq, LANES), lambda s, p: (0, s, p))
    return pl.pallas_call(
        _ctx_attn_kernel,
        grid=(nseq, W // LANES),
        in_specs=[blk, blk, blk],
        out_specs=pl.BlockSpec((1, seq, LANES), lambda s, p: (0, s, p)),
        out_shape=jax.ShapeDtypeStruct((1, nseq * seq, W), BF16),
        compiler_params=_cparams(("parallel", "parallel")),
        name="ctx_attention",
    )(q, k, v)


def _na_bias_table(rpb):
    H = rpb.shape[0]
    c = np.arange(GRID_W)[:, None]
    ck = np.arange(GRID_W)[None, :]
    wsc = np.clip(c - NA_KW // 2, 0, GRID_W - NA_KW)
    cvalid = (ck >= wsc) & (ck < wsc + NA_KW)
    dc = np.clip(ck - c + NA_KW - 1, 0, 2 * NA_KW - 2)
    bias_c = jnp.where(jnp.asarray(cvalid)[None, None], rpb.astype(F32)[:, :, dc], NEG_BIG)
    strips = [bias_c[:, d:d + NA_KH].transpose(0, 2, 1, 3).reshape(H, GRID_W, NA_KH * GRID_W) for d in range(NA_KH)]
    return jnp.stack(strips, axis=1)


def _na_attn_kernel(q_ref, k_ref, v_ref, kc_ref, vc_ref, ts_ref, o_ref, pc_sc, *, rows):
    i = pl.program_id(2)
    nq = NA_RB * GRID_W
    nkl = NA_KH * GRID_W
    q2, first = _pair_stack(q_ref[0])
    kc = kc_ref[0, 0].astype(BF16)
    vc = vc_ref[0, 0].astype(BF16)
    s_c = _dot_nt(q2, kc)
    o_loc, l_inv = [], []
    for qr in range(NA_RB):
        r = NA_RB * i + qr
        rs = jnp.clip(r - NA_KH // 2, 0, rows - NA_KH)
        d0 = rs - r + NA_KH - 1
        start = pl.multiple_of(rs * GRID_W, GRID_W)
        a, b = qr * GRID_W, nq + qr * GRID_W
        q2r = jnp.concatenate([q2[a:a + GRID_W], q2[b:b + GRID_W]], axis=0)
        bias = jnp.concatenate([ts_ref[0, d0], ts_ref[1, d0]], axis=0)
        s_l = _dot_nt(q2r, k_ref[0, pl.ds(start, nkl), :]) + bias
        s_cr = jnp.concatenate([s_c[a:a + GRID_W], s_c[b:b + GRID_W]], axis=0)
        m = jnp.maximum(jnp.max(s_l, axis=-1, keepdims=True), jnp.max(s_cr, axis=-1, keepdims=True))
        p_l = jnp.exp(s_l - m)
        p_c = jnp.exp(s_cr - m)
        l_inv.append(1.0 / (jnp.sum(p_l, axis=-1, keepdims=True) + jnp.sum(p_c, axis=-1, keepdims=True)))
        o_loc.append(_dot(p_l.astype(BF16), v_ref[0, pl.ds(start, nkl), :]))
        pc_sc[2 * a:2 * a + 2 * GRID_W, :] = p_c.astype(BF16)
    o_ctx = _dot(pc_sc[...], vc)
    for qr in range(NA_RB):
        a = qr * GRID_W
        o2 = (o_loc[qr] + o_ctx[2 * a:2 * a + 2 * GRID_W]) * l_inv[qr]
        o_ref[0, a:a + GRID_W, :] = jnp.where(first, o2[:GRID_W], o2[GRID_W:]).astype(BF16)


def _na_attention(q, k, v, cache_k, cache_v, layer, tab):
    R, NT, W = q.shape
    nreq = R - 1
    rows = NT // GRID_W
    assert rows % NA_RB == 0 and rows >= NA_KH
    nb = rows // NA_RB
    past = cache_k.shape[2]
    nq = NA_RB * GRID_W
    kern = functools.partial(_na_attn_kernel, rows=rows)
    slab = pl.BlockSpec((1, NT, LANES), lambda b, p, i: (b + 1, 0, p))
    ctx = pl.BlockSpec((1, 1, past, LANES), lambda b, p, i: (b, layer, 0, p))
    return pl.pallas_call(
        kern,
        grid=(nreq, W // LANES, nb),
        in_specs=[
            pl.BlockSpec((1, nq, LANES), lambda b, p, i: (b + 1, i, p)),
            slab, slab, ctx, ctx,
            pl.BlockSpec((2, NA_KH, GRID_W, NA_KH * GRID_W), lambda b, p, i: (p, 0, 0, 0)),
        ],
        out_specs=pl.BlockSpec((1, nq, LANES), lambda b, p, i: (b, i, p)),
        out_shape=jax.ShapeDtypeStruct((nreq, NT, W), BF16),
        scratch_shapes=[pltpu.VMEM((2 * nq, past), BF16)],
        compiler_params=_cparams(("parallel", "parallel", "arbitrary"), VMEM_LIMIT),
        name="na_attention",
    )(q, k, v, cache_k, cache_v, tab)


def _hgrn_chunk(q, v, lf, st, rev):
    C, HW = q.shape
    nh = HW // HEAD_DIM
    nsub = C // HG_SUB
    row = lax.broadcasted_iota(I32, (C, 1), 0)
    tau = (C - 1 - row) if rev else row
    b = lf
    d = 1
    while d < C:
        shifted = pltpu.roll(b, (C - d) if rev else d, axis=0)
        b = b + jnp.where(tau >= d, shifted, 0.0)
        d *= 2
    kk = 1.0 - jnp.exp(lf)
    last = 0 if rev else C - 1
    b_last = b[last:last + 1, :]
    q_inter = q * jnp.exp(b)
    k_end = kk * jnp.exp(b_last - b)

    lane_head = lax.broadcasted_iota(I32, (1, HW), 1) >> LOG_HEAD_DIM
    vb = v.astype(BF16)
    a_rows = []
    starts = []
    for i in range(nsub):
        mid_tau = HG_SUB * i + HG_SUB // 2 - 1
        mid = (C - 1 - mid_tau) if rev else mid_tau
        b_mid = b[mid:mid + 1, :]
        ps = (C - HG_SUB * (i + 1)) if rev else HG_SUB * i
        starts.append(ps)
        qs = q[ps:ps + HG_SUB, :] * jnp.exp(jnp.minimum(b[ps:ps + HG_SUB, :] - b_mid, HG_CLAMP))
        ks = (kk * jnp.exp(jnp.minimum(b_mid - b, HG_CLAMP))).astype(BF16)
        lhs = jnp.concatenate([jnp.where(lane_head == h, qs, 0.0) for h in range(nh)], axis=0).astype(BF16)
        a = _dot_nt(lhs, ks)
        t_sub = lax.broadcasted_iota(I32, (nh * HG_SUB, C), 0) & (HG_SUB - 1)
        t_tau = HG_SUB * i + ((HG_SUB - 1 - t_sub) if rev else t_sub)
        s_idx = lax.broadcasted_iota(I32, (nh * HG_SUB, C), 1)
        s_tau = (C - 1 - s_idx) if rev else s_idx
        a_rows.append(jnp.where(s_tau <= t_tau, a, 0.0).astype(BF16))
    av = _dot(jnp.concatenate(a_rows, axis=0), vb)
    pieces = [None] * nsub
    for i in range(nsub):
        acc = jnp.zeros((HG_SUB, HW), F32)
        for h in range(nh):
            r0 = (i * nh + h) * HG_SUB
            acc = acc + jnp.where(lane_head == h, av[r0:r0 + HG_SUB, :], 0.0)
        pieces[starts[i] // HG_SUB] = acc
    o = jnp.concatenate(pieces, axis=0) + _dot_nt(q_inter.astype(BF16), st.astype(BF16))

    rh = lax.broadcasted_iota(I32, (HW, HW), 0) >> LOG_HEAD_DIM
    ch = lax.broadcasted_iota(I32, (HW, HW), 1) >> LOG_HEAD_DIM
    upd = _dot_tn(vb, k_end.astype(BF16))
    st_new = st * jnp.exp(b_last) + jnp.where(rh == ch, upd, 0.0)
    return o, st_new


def _hgrn_kernel(*refs, has_init, ns):
    seqs = [refs[6 * k:6 * k + 6] for k in range(ns)]
    rest = refs[6 * ns:]
    if has_init:
        s0, of, ob, sout, st = rest
    else:
        of, ob, sout, st = rest
        s0 = None
    c = pl.program_id(1)

    @pl.when(c == 0)
    def _():
        if has_init:
            st[...] = s0[...]
        else:
            st[...] = jnp.zeros_like(st)

    for k, (qf, vf, lff, qb, vb_, lfb) in enumerate(seqs):
        o, s_new = _hgrn_chunk(qf[0], vf[0], lff[0], st[k, 0], rev=False)
        of[k] = o
        st[k, 0] = s_new
        o, s_new = _hgrn_chunk(qb[0], vb_[0], lfb[0], st[k, 1], rev=True)
        ob[k] = o
        st[k, 1] = s_new

    @pl.when(c == pl.num_programs(1) - 1)
    def _():
        sout[...] = st[...]


def _hgrn(hp, nseq, n, s0=None):
    R, NT, W6 = hp.shape
    HW = W6 // 6
    nc = n // HG_CHUNK
    latent = s0 is not None
    ns = 2 if latent and nseq % 2 == 0 else 1

    def in_spec(colblk, back, k):
        def index(g, c):
            cc = nc - 1 - c if back else c
            return (1 + ns * g + k, cc, colblk) if latent else (0, g * nc + cc, colblk)
        return pl.BlockSpec((1, HG_CHUNK, HW), index)

    def out_spec(back):
        def index(g, c):
            cc = nc - 1 - c if back else c
            return (g, cc, 0) if latent else (0, g * nc + cc, 0)
        return pl.BlockSpec((ns, HG_CHUNK, HW), index)

    in_specs, args = [], []
    for k in range(ns):
        in_specs += [in_spec(0, False, k), in_spec(1, False, k), in_spec(2, False, k),
                     in_spec(0, True, k), in_spec(1, True, k), in_spec(3, True, k)]
        args += [hp] * 6
    state_spec = pl.BlockSpec((ns, 2, HW, HW), lambda g, c: (g, 0, 0, 0))
    if latent:
        in_specs.append(state_spec)
        args.append(s0)
    nout = nseq if latent else 1
    return pl.pallas_call(
        functools.partial(_hgrn_kernel, has_init=latent, ns=ns),
        grid=(nseq // ns, nc),
        in_specs=in_specs,
        out_specs=[out_spec(False), out_spec(True), state_spec],
        out_shape=[
            jax.ShapeDtypeStruct((nout, NT, HW), F32),
            jax.ShapeDtypeStruct((nout, NT, HW), F32),
            jax.ShapeDtypeStruct((nseq, 2, HW, HW), F32),
        ],
        scratch_shapes=[pltpu.VMEM((ns, 2, HW, HW), F32)],
        compiler_params=_cparams(("parallel", "arbitrary")),
        name="hgrn_scan",
    )(*args)


def _pool_kernel(a_ref, w_ref, sc_ref, o_ref):
    a = a_ref[0]
    n, W = a.shape
    t = lax.broadcasted_iota(I32, (n, 1), 0)

    def shift(x, d):
        src = t - d
        return jnp.where((src >= 0) & (src < n), pltpu.roll(x, d % n, axis=0), 0.0)

    trail = [a]
    fwd = [a]
    for j in range(len(POOL_HALF) - 1):
        h = POOL_HALF[j]
        trail.append(trail[j] + shift(trail[j], h))
        fwd.append(fwd[j] + shift(fwd[j], -h))
    grp = lax.broadcasted_iota(I32, (1, W), 1) >> LOG_HEAD_DIM
    win = jnp.zeros_like(a)
    half = jnp.zeros((1, W), I32)
    for j, h in enumerate(POOL_HALF):
        win = jnp.where(grp == j, shift(trail[j], 1) + fwd[j], win)
        half = jnp.where(grp == j, h, half)
    cnt = jnp.minimum(t + half, n) - jnp.maximum(t - half, 0)
    p = win / cnt.astype(F32) - a
    o_ref[0] = (_dot(p.astype(BF16), w_ref[...]) * sc_ref[...]).astype(BF16)


def _pool(hp, nseq, n, w_bd, scale, latent):
    R, NT, W6 = hp.shape
    W = W6 // 6
    return pl.pallas_call(
        _pool_kernel,
        grid=(nseq,),
        in_specs=[
            pl.BlockSpec((1, n, W), (lambda s: (s + 1, 0, 5)) if latent else (lambda s: (0, s, 5))),
            pl.BlockSpec((W, W), lambda s: (0, 0)),
            pl.BlockSpec((1, W), lambda s: (0, 0)),
        ],
        out_specs=pl.BlockSpec((1, n, W), (lambda s: (s, 0, 0)) if latent else (lambda s: (0, s, 0))),
        out_shape=jax.ShapeDtypeStruct((nseq if latent else 1, NT, W), BF16),
        compiler_params=_cparams(("parallel",), VMEM_LIMIT),
        name="pool_mixer",
    )(hp, w_bd, scale)


def _outproj_kernel(xa_ref, xb_ref, na_a, na_b, of_a, of_b, ob_a, ob_b, op_a, op_b, sg_ref,
                    wo_ref, hgn_ref, seg_ref, g1_ref, gpm_ref, gpf_ref, sc2_ref, sh2_ref, wrh_ref, wrl_ref,
                    x1_ref, u2_ref, lg_ref, *, na_w, hg_w):
    tm, d = x1_ref.shape[1:]
    nchunk = d // LANES
    nsub = 2
    sub = tm // nsub
    first = pl.program_id(0) == 0
    for h in range(nsub):
        rows = slice(h * sub, (h + 1) * sub)

        def pick(a_ref, b_ref):
            return jnp.where(first, a_ref[0, rows, :], b_ref[0, rows, :])

        o = pick(of_a, of_b) + pick(ob_a, ob_b)
        hi, lo = _split_bf16(o * o)
        ms = _dot(hi, seg_ref[...]) + _dot(lo, seg_ref[...])
        ohg = o * lax.rsqrt(ms + EPS) * hgn_ref[...] * sg_ref[0, rows, :]
        mix = (_dot(pick(na_a, na_b), wo_ref[0:na_w, :])
               + _dot(ohg.astype(BF16), wo_ref[na_w:na_w + hg_w, :])
               + _dot(pick(op_a, op_b), wo_ref[na_w + hg_w:, :]))
        x1 = pick(xa_ref, xb_ref) + g1_ref[0] * (_rms(mix) * gpm_ref[...])
        x1_ref[0, rows, :] = x1
        u2 = _rms(x1) * gpf_ref[...] * (1.0 + sc2_ref[0]) + sh2_ref[0]
        for c in range(nchunk):
            u2_ref[0, pl.ds(h * sub * nchunk + c, sub, stride=nchunk), :] = u2[:, c * LANES:(c + 1) * LANES]
        uh, ul = _split_bf16(u2)
        lg_ref[0, :, rows] = _dot_nt(wrh_ref[...], uh) + _dot_nt(wrl_ref[...], uh) + _dot_nt(wrh_ref[...], ul)


def _outproj(x, ona, of, ob, opool, hp, wo_bf, hg_norm, seg, g1, gpm, gpf, sc2, sh2, wr_hi, wr_lo):
    _, NT, D = x[0].shape
    R = x[1].shape[0] + 1
    na_w, hg_w = D // 2, D // 4
    E = wr_hi.shape[0]
    tm = TOK_TILE
    nb = NT // tm
    kern = functools.partial(_outproj_kernel, na_w=na_w, hg_w=hg_w)
    tok = lambda r, i: (r, i, 0)
    per_req = lambda r, i: (r, 0, 0)
    const2 = lambda r, i: (0, 0)
    pairs = (_stacked_x_specs(tm, D, nb) + _stacked_x_specs(tm, na_w, nb) + _stacked_x_specs(tm, hg_w, nb)
             + _stacked_x_specs(tm, hg_w, nb) + _stacked_x_specs(tm, hg_w, nb))
    return pl.pallas_call(
        kern,
        grid=(R, nb),
        in_specs=pairs + [
            pl.BlockSpec((1, tm, hg_w), lambda r, i: (r, i, 4)),
            pl.BlockSpec((D, D), const2),
            pl.BlockSpec((1, hg_w), const2),
            pl.BlockSpec((hg_w, hg_w), const2),
            pl.BlockSpec((1, 1, D), per_req),
            pl.BlockSpec((1, D), const2),
            pl.BlockSpec((1, D), const2),
            pl.BlockSpec((1, 1, D), per_req),
            pl.BlockSpec((1, 1, D), per_req),
            pl.BlockSpec((E, D), const2),
            pl.BlockSpec((E, D), const2),
        ],
        out_specs=[
            pl.BlockSpec((1, tm, D), tok),
            pl.BlockSpec((1, tm * (D // LANES), LANES), tok),
            pl.BlockSpec((1, E, tm), lambda r, i: (r, 0, i)),
        ],
        out_shape=[
            jax.ShapeDtypeStruct((R, NT, D), F32),
            jax.ShapeDtypeStruct((R, NT * (D // LANES), LANES), F32),
            jax.ShapeDtypeStruct((R, E, NT), F32),
        ],
        compiler_params=_cparams(("parallel", "parallel"), VMEM_LIMIT),
        name="outproj_router",
    )(*x, *ona, *of, *ob, *opool, hp, wo_bf, hg_norm, seg, g1, gpm, gpf, sc2, sh2, wr_hi, wr_lo)


def _prefix_lanes(mask_f, tri):
    rows, n = mask_f.shape
    T = tri.shape[0]
    outs = []
    carry = jnp.zeros((rows, 1), F32)
    for j in range(n // T):
        seg = mask_f[:, j * T:(j + 1) * T]
        pre = _dot(seg.astype(BF16), tri) + carry
        outs.append(pre)
        carry = pre[:, T - 1:T]
    return outs[0] if len(outs) == 1 else jnp.concatenate(outs, axis=1)


def _router_kernel(lg_ref, tri_ref, idx_ref, aff_ref, rank_sc, *, cap, nrow_groups):
    if nrow_groups == 1:
        lg = lg_ref[0]
        ex = jnp.exp(lg - jnp.max(lg, axis=0, keepdims=True))
        aff = ex / jnp.sum(ex, axis=0, keepdims=True)
    else:
        lg = lg_ref[...]
        ex = jnp.exp(lg - jnp.max(lg, axis=0, keepdims=True))
        aff = ex / jnp.sum(ex, axis=0, keepdims=True)
        aff = aff.reshape(lg.shape[0] * lg.shape[1], lg.shape[2])
    rows, n = aff.shape
    aff_ref[...] = aff.reshape(aff_ref.shape)
    thr = jnp.zeros((rows, 1), I32)
    for bit in range(30, -1, -1):
        cand = thr | (1 << bit)
        cnt = jnp.sum((aff >= pltpu.bitcast(cand, F32)).astype(F32), axis=-1, keepdims=True)
        thr = jnp.where(cnt >= cap, cand, thr)
    thr_f = pltpu.bitcast(thr, F32)
    gt = aff > thr_f
    eq = aff == thr_f
    need = cap - jnp.sum(gt.astype(F32), axis=-1, keepdims=True)
    tri = tri_ref[...]
    eq_f = eq.astype(F32)
    eq_before = _prefix_lanes(eq_f, tri) - eq_f
    sel = gt | (eq & (eq_before < need))
    rank_sc[...] = _prefix_lanes(sel.astype(F32), tri)

    ncol = idx_ref.shape[-1]
    jcol = lax.broadcasted_iota(I32, (cap, 1), 0).astype(F32)
    lane = lax.broadcasted_iota(I32, (1, ncol), 1)

    def body(r, acc):
        rk = rank_sc[pl.ds(r, 1), :]
        pos = jnp.sum((rk <= jcol).astype(F32), axis=-1, keepdims=True)
        return jnp.where(lane == r, pos, acc)

    idx = lax.fori_loop(0, rows, body, jnp.zeros((cap, ncol), F32))
    idx_ref[...] = idx.astype(I32).reshape(idx_ref.shape)


def _router_seg_kernel(lg_ref, tri_ref, mexc_ref, minc_ref, idx_ref, aff_ref, loc_sc, *, cap):
    lg = lg_ref[0]
    E, S, _ = lg.shape
    rows = E * S
    log_s = S.bit_length() - 1
    ex = jnp.exp(lg - jnp.max(lg, axis=0, keepdims=True))
    aff = ex / jnp.sum(ex, axis=0, keepdims=True)
    aff_ref[0] = aff
    thr = jnp.zeros((E, 1, 1), I32)
    for bit in range(30, -1, -1):
        cand = thr | (1 << bit)
        cnt = jnp.sum((aff >= pltpu.bitcast(cand, F32)).astype(F32), axis=(1, 2), keepdims=True)
        thr = jnp.where(cnt >= cap, cand, thr)
    thr_f = pltpu.bitcast(thr, F32)
    gt = aff > thr_f
    eq = aff == thr_f
    need = cap - jnp.sum(gt.astype(F32), axis=(1, 2), keepdims=True)
    tri = tri_ref[...]
    eq2 = eq.astype(F32).reshape(rows, LANES)
    eq_loc = _dot(eq2.astype(BF16), tri)
    eq_tot = jnp.broadcast_to(eq_loc[:, LANES - 1:LANES], (rows, LANES)).astype(BF16)
    eq_before = (eq_loc - eq2 + _dot(mexc_ref[...], eq_tot)).reshape(E, S, LANES)
    sel = (gt | (eq & (eq_before < need))).astype(BF16).reshape(rows, LANES)
    loc_sc[...] = _dot(sel, tri).astype(BF16)
    tot_l = _dot_nt(jnp.ones((8, LANES), BF16), sel)
    pinc_l = _dot(tot_l.astype(BF16), minc_ref[...])[0:1]
    pexc_l = pinc_l - tot_l[0:1]
    jcol = lax.broadcasted_iota(I32, (cap, 1), 0).astype(F32)
    lane_e = lax.broadcasted_iota(I32, (1, rows), 1) >> log_s
    lane = lax.broadcasted_iota(I32, (1, idx_ref.shape[-1]), 1)

    def body(e, acc):
        mine = lane_e == e
        inside = mine & (pexc_l <= jcol) & (jcol < pinc_l)
        nfull = jnp.sum((mine & (pinc_l <= jcol)).astype(F32), axis=-1, keepdims=True)
        before = jnp.sum(jnp.where(inside, pexc_l, 0.0), axis=-1, keepdims=True)
        rsel = _dot(inside.astype(BF16), loc_sc[...])
        cnt = jnp.sum((rsel <= jcol - before).astype(F32), axis=-1, keepdims=True)
        return jnp.where(lane == e, LANES * nfull + cnt, acc)

    idx = lax.fori_loop(0, E, body, jnp.zeros((cap, idx_ref.shape[-1]), F32))
    idx_ref[0] = idx.astype(I32)


def _router_seg(logits, cap):
    G, E, N = logits.shape
    S = N // LANES
    assert S & (S - 1) == 0
    rows = E * S
    tri = jnp.asarray(np.triu(np.ones((LANES, LANES), np.float32)), BF16)
    mexc = jnp.asarray(np.kron(np.eye(E), np.tril(np.ones((S, S)), -1)), BF16)
    minc = jnp.asarray(np.kron(np.eye(E), np.triu(np.ones((S, S)))), BF16)
    idx, aff = pl.pallas_call(
        functools.partial(_router_seg_kernel, cap=cap),
        grid=(G,),
        in_specs=[
            pl.BlockSpec((1, E, S, LANES), lambda g: (g, 0, 0, 0)),
            pl.BlockSpec((LANES, LANES), lambda g: (0, 0)),
            pl.BlockSpec((rows, rows), lambda g: (0, 0)),
            pl.BlockSpec((rows, rows), lambda g: (0, 0)),
        ],
        out_specs=[pl.BlockSpec((1, cap, LANES), lambda g: (g, 0, 0)),
                   pl.BlockSpec((1, E, S, LANES), lambda g: (g, 0, 0, 0))],
        out_shape=[jax.ShapeDtypeStruct((G, cap, LANES), I32), jax.ShapeDtypeStruct((G, E, S, LANES), F32)],
        scratch_shapes=[pltpu.VMEM((rows, LANES), BF16)],
        compiler_params=_cparams(("parallel",), VMEM_LIMIT),
        name="router_topk_seg",
    )(logits.reshape(G, E, S, LANES), tri, mexc, minc)
    return idx, aff.reshape(G, E, N)


def _router(logits, nq, n, cap, per_step_requests):
    G, E, NTOT = logits.shape
    rows = E * nq
    ncol = max(LANES, rows)
    T = min(n, 2 * LANES)
    tri = jnp.asarray(np.triu(np.ones((T, T), np.float32)), BF16)
    kern = functools.partial(_router_kernel, cap=cap, nrow_groups=nq)
    if nq == 1:
        lg_spec = pl.BlockSpec((1, E, n), lambda g: (g, 0, 0))
        lg_in = logits
        aff_spec = pl.BlockSpec((1, E, n), lambda g: (g, 0, 0))
        aff_shape = jax.ShapeDtypeStruct((G, E, n), F32)
    else:
        assert G == 1
        lg_in = logits.reshape(E, nq, n)
        lg_spec = pl.BlockSpec((E, nq, n), lambda g: (0, 0, 0))
        aff_spec = pl.BlockSpec((rows, n), lambda g: (0, 0))
        aff_shape = jax.ShapeDtypeStruct((rows, n), F32)
    return pl.pallas_call(
        kern,
        grid=(G,),
        in_specs=[lg_spec, pl.BlockSpec((T, T), lambda g: (0, 0))],
        out_specs=[pl.BlockSpec((1, cap, ncol), lambda g: (g, 0, 0)), aff_spec],
        out_shape=[jax.ShapeDtypeStruct((G, cap, ncol), I32), aff_shape],
        scratch_shapes=[pltpu.VMEM((rows, n), F32)],
        compiler_params=_cparams(("parallel",), VMEM_LIMIT),
        name="router_topk",
    )(lg_in, tri)


def _ffn_kernel(idx_ref, affp_ref, affc_ref, u_ref, wg_ref, wu_ref, wd_ref, y_hbm, xs, outs, y_acc, sem,
                *, cap, nchunk):
    r = pl.program_id(0)
    e = pl.program_id(1)
    ne = pl.num_programs(1)
    group = 8
    cur = e & 1
    oth = 1 - cur

    def tile(i):
        if isinstance(i, int):
            return pl.ds(i * nchunk, nchunk)
        return pl.ds(pl.multiple_of(i * nchunk, nchunk), nchunk)

    def gather_group(expert, buf, j0):
        for u in range(group):
            xs[buf, tile(j0 + u), :] = u_ref[0, tile(idx_ref[0, 0, expert * cap + j0 + u]), :]

    def scatter_group(expert, buf, gate_ref, scale, j0):
        ts = [idx_ref[0, 0, expert * cap + j0 + u] for u in range(group)]
        new = [y_acc[tile(t), :] + outs[buf, tile(j0 + u), :] * (gate_ref[0, 0, 0, t] * scale)
               for u, t in enumerate(ts)]
        for t, val in zip(ts, new):
            y_acc[tile(t), :] = val

    @pl.when(e == 0)
    def _():
        y_acc[...] = jnp.zeros_like(y_acc)
        outs[...] = jnp.zeros_like(outs)

        def body(jb, carry):
            gather_group(0, 0, jb * group)
            return carry

        lax.fori_loop(0, cap // group, body, 0)

    e_next = jnp.minimum(e + 1, ne - 1)
    e_prev = jnp.maximum(e - 1, 0)
    prev_scale = jnp.where(e > 0, 1.0, 0.0)
    xb = _load_token_tiles(xs.at[cur], cap, nchunk).astype(BF16)
    ff = wg_ref.shape[-1]
    nsplit = 4
    fb = ff // nsplit
    per = cap // nsplit
    o = None
    for n in range(nsplit):
        cols = slice(n * fb, (n + 1) * fb)
        hid = _silu(_dot(xb, wg_ref[0, 0, :, cols])) * _dot(xb, wu_ref[0, 0, :, cols])
        part = _dot(hid.astype(BF16), wd_ref[0, 0, cols, :])
        o = part if o is None else o + part
        for j0 in range(n * per, (n + 1) * per, group):
            gather_group(e_next, oth, j0)
            scatter_group(e_prev, oth, affp_ref, prev_scale, j0)
    for c in range(nchunk):
        outs[cur, pl.ds(c, cap, stride=nchunk), :] = o[:, c * LANES:(c + 1) * LANES]

    @pl.when(e == ne - 1)
    def _():
        def body(jb, carry):
            scatter_group(e, cur, affc_ref, 1.0, jb * group)
            return carry

        lax.fori_loop(0, cap // group, body, 0)
        cp = pltpu.make_async_copy(y_acc, y_hbm.at[r], sem)
        cp.start()
        cp.wait()


def _expert_ffn(u2, idx, aff, wg, wu, wd, layer):
    _, E, D, FF = wg.shape
    nchunk = D // LANES
    R, rows, _ = u2.shape
    NT = rows // nchunk
    cap = idx.shape[-1]
    return pl.pallas_call(
        functools.partial(_ffn_kernel, cap=cap, nchunk=nchunk),
        grid=(R, E),
        in_specs=[
            pl.BlockSpec((1, 1, E * cap), lambda r, e: (r, 0, 0), memory_space=pltpu.SMEM),
            pl.BlockSpec((1, 1, 1, NT), lambda r, e: (r, jnp.maximum(e - 1, 0), 0, 0), memory_space=pltpu.SMEM),
            pl.BlockSpec((1, 1, 1, NT), lambda r, e: (r, e, 0, 0), memory_space=pltpu.SMEM),
            pl.BlockSpec((1, rows, LANES), lambda r, e: (r, 0, 0), pipeline_mode=pl.Buffered(1)),
            pl.BlockSpec((1, 1, D, FF), lambda r, e: (layer, e, 0, 0)),
            pl.BlockSpec((1, 1, D, FF), lambda r, e: (layer, e, 0, 0)),
            pl.BlockSpec((1, 1, FF, D), lambda r, e: (layer, e, 0, 0)),
        ],
        out_specs=pl.BlockSpec(memory_space=pl.ANY),
        out_shape=jax.ShapeDtypeStruct((R, rows, LANES), F32),
        scratch_shapes=[
            pltpu.VMEM((2, cap * nchunk, LANES), F32),
            pltpu.VMEM((2, cap * nchunk, LANES), F32),
            pltpu.VMEM((rows, LANES), F32),
            pltpu.SemaphoreType.DMA(()),
        ],
        compiler_params=_cparams(("arbitrary", "arbitrary"), 60 * 1024 * 1024),
        name="expert_ffn",
    )(idx.reshape(R, 1, E * cap), aff.reshape(R, E, 1, NT), aff.reshape(R, E, 1, NT), u2, wg, wu, wd)


def _resid_kernel(x_ref, ff_ref, g2_ref, gn_ref, o_ref):
    n, d = x_ref.shape[1:]
    ff = _load_token_tiles(ff_ref.at[0], n, d // LANES)
    o_ref[0] = x_ref[0] + g2_ref[0] * (_rms(ff) * gn_ref[...])


def _ffn_residual(x1, ff, g2, gn, r0, nr):
    _, NT, D = x1.shape
    tm = TOK_TILE
    tok = lambda r, i: (r0 + r, i, 0)
    return pl.pallas_call(
        _resid_kernel,
        grid=(nr, NT // tm),
        in_specs=[
            pl.BlockSpec((1, tm, D), tok),
            pl.BlockSpec((1, tm * (D // LANES), LANES), tok),
            pl.BlockSpec((1, 1, D), lambda r, i: (r0 + r, 0, 0)),
            pl.BlockSpec((1, D), lambda r, i: (0, 0)),
        ],
        out_specs=pl.BlockSpec((1, tm, D), lambda r, i: (r, i, 0)),
        out_shape=jax.ShapeDtypeStruct((nr, NT, D), F32),
        compiler_params=_cparams(("parallel", "parallel")),
        name="ffn_residual",
    )(x1, ff, g2, gn)


def kernel(x_prompt, x_sample, cache_na_k, cache_na_v, state_hgrn, c, c_ctx, ada_w, ada_b,
           norm_pre_mix, norm_post_mix, norm_pre_ffn, norm_post_ffn, w_in, w_out, na_rpb,
           hg_lb, hg_norm, pool_w, pool_scale, w_router, w_gate, w_up, w_down):
    B, SEQ, D = x_prompt.shape
    NB, NT, _ = x_sample.shape
    L = ada_w.shape[0]
    assert B * SEQ == NT, "context tokens are stacked as one request of the latent sequence length"
    R = NB + 1
    na_w, hg_w = D // 2, D // 4
    H = na_w // HEAD_DIM
    nh = hg_w // HEAD_DIM
    E = w_router.shape[-1]
    past = cache_na_k.shape[2]
    cap_ctx = 2 * SEQ // E
    cap_lat = 2 * NT // E
    assert B * cap_ctx == cap_lat

    xa, xb = x_prompt.reshape(1, NT, D), x_sample
    RP = 16
    conds = jnp.zeros((RP, D), F32).at[0].set(c_ctx).at[1:R].set(c)
    mod = _modulation(conds, ada_w, ada_b)[:, :R].reshape(L, R, 6, 1, D)

    lb = jnp.cumsum(jax.nn.softmax(hg_lb.astype(F32), axis=0), axis=0)
    lb = lb - lb[:1]
    log_lb, log1m_lb = jnp.log(lb), jnp.log1p(-lb)

    ck = cache_na_k.reshape(NB, L, past, na_w)
    cv = cache_na_v.reshape(NB, L, past, na_w)
    seg = jnp.asarray(np.kron(np.eye(nh), np.full((HEAD_DIM, HEAD_DIM), 1.0 / HEAD_DIM)), BF16)
    eye_h = jnp.eye(nh, dtype=F32)
    s0_all = jnp.einsum('bldhkv,hg->bldhvgk', state_hgrn.astype(F32), eye_h).reshape(NB, L, 2, hg_w, hg_w)

    wg_bf, wu_bf, wd_bf = w_gate.astype(BF16), w_up.astype(BF16), w_down.astype(BF16)
    new_k, new_v, new_s = [], [], []
    for l in range(L):
        sh1, sc1, g1, sh2, sc2, g2 = (mod[l, :, j] for j in range(6))
        w_bf = w_in[l].astype(BF16)
        g0 = 3 * na_w + hg_w
        w_gate_lo = (w_in[l][:, g0:g0 + 2 * hg_w] - w_bf[:, g0:g0 + 2 * hg_w].astype(F32)).astype(BF16)
        q, k, v, kvf, hp = _inproj(xa, xb, norm_pre_mix[l][None], sc1, sh1, w_bf, w_gate_lo,
                                   log_lb[l], log1m_lb[l])
        new_k.append(kvf[:NT, :na_w].reshape(B, SEQ, H, HEAD_DIM))
        new_v.append(kvf[:NT, na_w:].reshape(B, SEQ, H, HEAD_DIM))

        tab = _na_bias_table(na_rpb[l])
        ona = (_ctx_attention(q, k, v, B, SEQ), _na_attention(q, k, v, ck, cv, l, tab))

        of_c, ob_c, st_c = _hgrn(hp, B, SEQ)
        of_l, ob_l, _ = _hgrn(hp, NB, NT, s0=s0_all[:, l])
        st_c = st_c.reshape(B, 2, nh, HEAD_DIM, nh, HEAD_DIM)
        new_s.append(jnp.einsum('bdhvgk,hg->bdhkv', st_c, eye_h))

        w_bd = jnp.einsum('gcd,gh->gchd', pool_w[l].astype(F32), jnp.eye(len(POOL_HALF), dtype=F32))
        w_bd = w_bd.reshape(hg_w, hg_w).astype(BF16)
        opool = (_pool(hp, B, SEQ, w_bd, pool_scale[l][None], False),
                 _pool(hp, NB, NT, w_bd, pool_scale[l][None], True))

        wr_hi, wr_lo = _split_bf16(w_router[l].T.astype(F32))
        x1, u2, logits = _outproj((xa, xb), ona, (of_c, of_l), (ob_c, ob_l), opool, hp, w_out[l].astype(BF16),
                                  hg_norm[l][None], seg, g1, norm_post_mix[l][None], norm_pre_ffn[l][None],
                                  sc2, sh2, wr_hi, wr_lo)

        idx_c, aff_c = _router(logits[:1], B, SEQ, cap_ctx, False)
        idx_l, aff_l = _router_seg(logits[1:], cap_lat)
        idx_c = idx_c[0, :, :E * B].reshape(cap_ctx, E, B).transpose(1, 2, 0)
        idx_c = (idx_c + (jnp.arange(B, dtype=I32) * SEQ)[None, :, None]).reshape(1, E, cap_lat)
        idx = jnp.concatenate([idx_c, idx_l[:, :, :E].transpose(0, 2, 1)], axis=0)
        aff = jnp.concatenate([aff_c.reshape(1, E, NT), aff_l], axis=0)

        ff = _expert_ffn(u2, idx, aff, wg_bf, wu_bf, wd_bf, l)
        xa = _ffn_residual(x1, ff, g2, norm_post_ffn[l][None], 0, 1)
        xb = _ffn_residual(x1, ff, g2, norm_post_ffn[l][None], 1, NB)

    y_prompt = xa.reshape(B, SEQ, D)
    y_sample = xb
    new_cache_k = jnp.stack(new_k, axis=1)
    new_cache_v = jnp.stack(new_v, axis=1)
    new_state = jnp.stack(new_s, axis=1).astype(x_prompt.dtype)
    return (y_prompt, y_sample, new_cache_k, new_cache_v, new_state)
```

```python
import functools

import numpy as np
import jax
import jax.numpy as jnp
from jax import lax
from jax.experimental import pallas as pl
from jax.experimental.pallas import tpu as pltpu

F32 = jnp.float32
BF16 = jnp.bfloat16
I32 = jnp.int32

HEAD_DIM = 64
LOG_HEAD_DIM = 6
LANES = 128
GRID_W = 64
NA_KH = 8
NA_KW = 16
NA_RB = 8
NA_WR = 16
HG_CHUNK = 64
HG_SUB = 16
HG_CLAMP = 80.0
POOL_HALF = (1, 2, 4, 8)
N_EXPERTS = 16
EPS = 1e-6
NEG_BIG = -1e30
TOK_TILE = 512
VMEM_LIMIT = 56 * 1024 * 1024


def _cparams(sem, vmem=None):
    return pltpu.CompilerParams(dimension_semantics=sem, vmem_limit_bytes=vmem)


def _silu(x):
    return x / (1.0 + jnp.exp(-x))


def _dot(a, b):
    return jnp.dot(a, b, preferred_element_type=F32)


def _dot_nt(a, b):
    return lax.dot_general(a, b, (((1,), (1,)), ((), ())), preferred_element_type=F32)


def _dot_tn(a, b):
    return lax.dot_general(a, b, (((0,), (0,)), ((), ())), preferred_element_type=F32)


def _rms(x):
    return x * lax.rsqrt(jnp.mean(x * x, axis=-1, keepdims=True) + EPS)


def _store_token_tiles(ref, x):
    n, d = x.shape
    nchunk = d // LANES
    for c in range(nchunk):
        ref[0, pl.ds(c, n, stride=nchunk), :] = x[:, c * LANES:(c + 1) * LANES]


def _load_token_tiles(ref2d, n, nchunk):
    return jnp.concatenate([ref2d[pl.ds(c, n, stride=nchunk), :] for c in range(nchunk)], axis=1)


def _split_bf16(x):
    hi = x.astype(BF16)
    lo = (x - hi.astype(F32)).astype(BF16)
    return hi, lo


def _mod_kernel(c_ref, w_ref, b_ref, o_ref):
    ah, al = _split_bf16(_silu(c_ref[...]))
    wh, wl = _split_bf16(w_ref[0])
    o_ref[0] = _dot(ah, wh) + _dot(al, wh) + _dot(ah, wl) + b_ref[0]


def _modulation(conds, ada_w, ada_b):
    L, D, D6 = ada_w.shape
    RP = conds.shape[0]
    return pl.pallas_call(
        _mod_kernel,
        grid=(L, D6 // D),
        in_specs=[
            pl.BlockSpec((RP, D), lambda l, j: (0, 0)),
            pl.BlockSpec((1, D, D), lambda l, j: (l, 0, j)),
            pl.BlockSpec((1, 1, D), lambda l, j: (l, 0, j)),
        ],
        out_specs=pl.BlockSpec((1, RP, D), lambda l, j: (l, 0, j)),
        out_shape=jax.ShapeDtypeStruct((L, RP, D6), F32),
        compiler_params=_cparams(("parallel", "parallel")),
        name="modulation",
    )(conds, ada_w, ada_b.reshape(L, 1, D6))


def _log_gate(x, la, l1):
    y = l1 + jnp.minimum(x, 0.0) - jnp.log1p(jnp.exp(-jnp.abs(x)))
    return jnp.maximum(la, y) + jnp.log1p(jnp.exp(-jnp.abs(la - y)))


def _stacked_x(xa_ref, xb_ref):
    return jnp.where(pl.program_id(0) == 0, xa_ref[0], xb_ref[0])


def _stacked_x_specs(tm, D, nb):
    return [
        pl.BlockSpec((1, tm, D), lambda r, i: (0, jnp.where(r == 0, i, nb - 1), 0)),
        pl.BlockSpec((1, tm, D), lambda r, i: (jnp.maximum(r - 1, 0), jnp.where(r == 0, 0, i), 0)),
    ]


def _inproj_kernel(xa_ref, xb_ref, g_ref, sc_ref, sh_ref, w_ref, wlo_ref, la_ref, l1_ref,
                   q_ref, k_ref, v_ref, kvf_ref, hp_ref, *, na_w, hg_w):
    u = _rms(_stacked_x(xa_ref, xb_ref)) * g_ref[...] * (1.0 + sc_ref[0]) + sh_ref[0]
    uh, ul = _split_bf16(u)
    p = _dot(uh, w_ref[...])
    g0 = 3 * na_w + hg_w
    gates = p[:, g0:g0 + 2 * hg_w] + _dot(ul, w_ref[:, g0:g0 + 2 * hg_w]) + _dot(uh, wlo_ref[...])
    k = p[:, na_w:2 * na_w]
    v = p[:, 2 * na_w:3 * na_w]
    q_ref[0] = (p[:, :na_w] * (HEAD_DIM ** -0.5)).astype(BF16)
    k_ref[0] = k.astype(BF16)
    v_ref[0] = v.astype(BF16)
    kvf_ref[:, :na_w] = k
    kvf_ref[:, na_w:] = v
    o = 3 * na_w

    def col(j):
        return p[:, o + j * hg_w:o + (j + 1) * hg_w]

    hp_ref[0, :, 0 * hg_w:1 * hg_w] = _silu(col(0))
    hp_ref[0, :, 1 * hg_w:2 * hg_w] = col(3)
    hp_ref[0, :, 2 * hg_w:3 * hg_w] = _log_gate(gates[:, :hg_w], la_ref[0:1, :], l1_ref[0:1, :])
    hp_ref[0, :, 3 * hg_w:4 * hg_w] = _log_gate(gates[:, hg_w:], la_ref[1:2, :], l1_ref[1:2, :])
    hp_ref[0, :, 4 * hg_w:5 * hg_w] = _silu(col(4))
    hp_ref[0, :, 5 * hg_w:6 * hg_w] = col(5)


def _inproj(xa, xb, g, sc, sh, w_in_bf, w_gate_lo, log_lb, log1m_lb):
    _, NT, D = xa.shape
    R = xb.shape[0] + 1
    na_w, hg_w = D // 2, D // 4
    inw = w_in_bf.shape[1]
    tm = TOK_TILE
    nb = NT // tm
    kern = functools.partial(_inproj_kernel, na_w=na_w, hg_w=hg_w)
    tok = lambda r, i: (r, i, 0)
    per_req = lambda r, i: (r, 0, 0)
    const2 = lambda r, i: (0, 0)
    return pl.pallas_call(
        kern,
        grid=(R, nb),
        in_specs=_stacked_x_specs(tm, D, nb) + [
            pl.BlockSpec((1, D), const2),
            pl.BlockSpec((1, 1, D), per_req),
            pl.BlockSpec((1, 1, D), per_req),
            pl.BlockSpec((D, inw), const2),
            pl.BlockSpec((D, 2 * hg_w), const2),
            pl.BlockSpec((2, hg_w), const2),
            pl.BlockSpec((2, hg_w), const2),
        ],
        out_specs=[
            pl.BlockSpec((1, tm, na_w), tok),
            pl.BlockSpec((1, tm, na_w), tok),
            pl.BlockSpec((1, tm, na_w), tok),
            pl.BlockSpec((tm, 2 * na_w), lambda r, i: (jnp.where(r == 0, i, nb), 0)),
            pl.BlockSpec((1, tm, 6 * hg_w), tok),
        ],
        out_shape=[
            jax.ShapeDtypeStruct((R, NT, na_w), BF16),
            jax.ShapeDtypeStruct((R, NT, na_w), BF16),
            jax.ShapeDtypeStruct((R, NT, na_w), BF16),
            jax.ShapeDtypeStruct(((nb + 1) * tm, 2 * na_w), F32),
            jax.ShapeDtypeStruct((R, NT, 6 * hg_w), F32),
        ],
        compiler_params=_cparams(("arbitrary", "arbitrary"), VMEM_LIMIT),
        name="prenorm_inproj",
    )(xa, xb, g, sc, sh, w_in_bf, w_gate_lo, log_lb, log1m_lb)


def _pair_stack(q):
    first = lax.broadcasted_iota(I32, (1, LANES), 1) < HEAD_DIM
    zero = jnp.zeros_like(q)
    return jnp.concatenate([jnp.where(first, q, zero), jnp.where(first, zero, q)], axis=0), first


def _ctx_attn_kernel(q_ref, k_ref, v_ref, o_ref):
    n = q_ref.shape[1]
    q2, first = _pair_stack(q_ref[0])
    s = _dot_nt(q2, k_ref[0])
    p = jnp.exp(s - jnp.max(s, axis=-1, keepdims=True))
    l = jnp.sum(p, axis=-1, keepdims=True)
    o2 = _dot(p.astype(BF16), v_ref[0]) * (1.0 / l)
    o_ref[0] = jnp.where(first, o2[:n], o2[n:]).astype(BF16)


def _ctx_attention(q, k, v, nseq, seq):
    W = q.shape[-1]
    blk = pl.BlockSpec((1, seq, LANES), lambda s, p: (0, s, p))
    return pl.pallas_call(
        _ctx_attn_kernel,
        grid=(nseq, W // LANES),
        in_specs=[blk, blk, blk],
        out_specs=pl.BlockSpec((1, seq, LANES), lambda s, p: (0, s, p)),
        out_shape=jax.ShapeDtypeStruct((1, nseq * seq, W), BF16),
        compiler_params=_cparams(("parallel", "parallel")),
        name="ctx_attention",
    )(q, k, v)


def _na_bias_table(rpb):
    H = rpb.shape[0]
    c = np.arange(GRID_W)[:, None]
    ck = np.arange(GRID_W)[None, :]
    wsc = np.clip(c - NA_KW // 2, 0, GRID_W - NA_KW)
    cvalid = (ck >= wsc) & (ck < wsc + NA_KW)
    nd, nx = 2 * NA_KH - 1, 2 * NA_KW - 1
    xsel = np.where(cvalid, np.clip(ck - c + NA_KW - 1, 0, nx - 1), nx)
    col_onehot = jnp.asarray(np.eye(nx + 1, dtype=np.float32)[xsel])
    row_onehot = np.zeros((NA_KH, NA_KH, nd), np.float32)
    for d in range(NA_KH):
        row_onehot[d, np.arange(NA_KH), d + np.arange(NA_KH)] = 1.0
    rpb_ext = jnp.concatenate([rpb.astype(F32), jnp.full((H, nd, 1), NEG_BIG, F32)], axis=2)
    ts = jnp.einsum('hax,dja,ckx->hdcjk', rpb_ext, jnp.asarray(row_onehot), col_onehot,
                    precision=lax.Precision.HIGHEST)
    return ts.reshape(H, NA_KH, GRID_W, NA_KH * GRID_W)


def _na_attn_kernel(q_ref, k_ref, v_ref, kc_ref, vc_ref, ts_ref, o_ref, pc_sc, *, rows):
    i = pl.program_id(2)
    nq = NA_RB * GRID_W
    nkl = NA_KH * GRID_W
    q2, first = _pair_stack(q_ref[0])
    kc = kc_ref[0, 0].astype(BF16)
    vc = vc_ref[0, 0].astype(BF16)
    s_c = _dot_nt(q2, kc)
    o_loc, l_inv = [], []
    for qr in range(NA_RB):
        r = NA_RB * i + qr
        rs = jnp.clip(r - NA_KH // 2, 0, rows - NA_KH)
        d0 = rs - r + NA_KH - 1
        start = pl.multiple_of(rs * GRID_W, GRID_W)
        a, b = qr * GRID_W, nq + qr * GRID_W
        q2r = jnp.concatenate([q2[a:a + GRID_W], q2[b:b + GRID_W]], axis=0)
        bias = jnp.concatenate([ts_ref[0, d0], ts_ref[1, d0]], axis=0)
        s_l = _dot_nt(q2r, k_ref[0, pl.ds(start, nkl), :]) + bias
        s_cr = jnp.concatenate([s_c[a:a + GRID_W], s_c[b:b + GRID_W]], axis=0)
        m = jnp.maximum(jnp.max(s_l, axis=-1, keepdims=True), jnp.max(s_cr, axis=-1, keepdims=True))
        p_l = jnp.exp(s_l - m)
        p_c = jnp.exp(s_cr - m)
        l_inv.append(1.0 / (jnp.sum(p_l, axis=-1, keepdims=True) + jnp.sum(p_c, axis=-1, keepdims=True)))
        o_loc.append(_dot(p_l.astype(BF16), v_ref[0, pl.ds(start, nkl), :]))
        pc_sc[2 * a:2 * a + 2 * GRID_W, :] = p_c.astype(BF16)
    o_ctx = _dot(pc_sc[...], vc)
    for qr in range(NA_RB):
        a = qr * GRID_W
        o2 = (o_loc[qr] + o_ctx[2 * a:2 * a + 2 * GRID_W]) * l_inv[qr]
        o_ref[0, a:a + GRID_W, :] = jnp.where(first, o2[:GRID_W], o2[GRID_W:]).astype(BF16)


def _na_attention(q, k, v, cache_k, cache_v, layer, tab):
    R, NT, W = q.shape
    nreq = R - 1
    rows = NT // GRID_W
    assert rows % NA_RB == 0 and rows >= NA_KH
    nb = rows // NA_RB
    past = cache_k.shape[2]
    nq = NA_RB * GRID_W
    kern = functools.partial(_na_attn_kernel, rows=rows)
    slab = pl.BlockSpec((1, NT, LANES), lambda b, p, i: (b + 1, 0, p))
    ctx = pl.BlockSpec((1, 1, past, LANES), lambda b, p, i: (b, layer, 0, p))
    return pl.pallas_call(
        kern,
        grid=(nreq, W // LANES, nb),
        in_specs=[
            pl.BlockSpec((1, nq, LANES), lambda b, p, i: (b + 1, i, p)),
            slab, slab, ctx, ctx,
            pl.BlockSpec((2, NA_KH, GRID_W, NA_KH * GRID_W), lambda b, p, i: (p, 0, 0, 0)),
        ],
        out_specs=pl.BlockSpec((1, nq, LANES), lambda b, p, i: (b, i, p)),
        out_shape=jax.ShapeDtypeStruct((nreq, NT, W), BF16),
        scratch_shapes=[pltpu.VMEM((2 * nq, past), BF16)],
        compiler_params=_cparams(("parallel", "parallel", "arbitrary"), VMEM_LIMIT),
        name="na_attention",
    )(q, k, v, cache_k, cache_v, tab)


def _hgrn_chunk(q, v, lf, st, rev):
    C, HW = q.shape
    nh = HW // HEAD_DIM
    nsub = C // HG_SUB
    row = lax.broadcasted_iota(I32, (C, 1), 0)
    tau = (C - 1 - row) if rev else row
    b = lf
    d = 1
    while d < C:
        shifted = pltpu.roll(b, (C - d) if rev else d, axis=0)
        b = b + jnp.where(tau >= d, shifted, 0.0)
        d *= 2
    kk = 1.0 - jnp.exp(lf)
    last = 0 if rev else C - 1
    b_last = b[last:last + 1, :]
    q_inter = q * jnp.exp(b)
    k_end = kk * jnp.exp(b_last - b)

    lane_head = lax.broadcasted_iota(I32, (1, HW), 1) >> LOG_HEAD_DIM
    vb = v.astype(BF16)
    a_rows = []
    starts = []
    for i in range(nsub):
        mid_tau = HG_SUB * i + HG_SUB // 2 - 1
        mid = (C - 1 - mid_tau) if rev else mid_tau
        b_mid = b[mid:mid + 1, :]
        ps = (C - HG_SUB * (i + 1)) if rev else HG_SUB * i
        starts.append(ps)
        qs = q[ps:ps + HG_SUB, :] * jnp.exp(jnp.minimum(b[ps:ps + HG_SUB, :] - b_mid, HG_CLAMP))
        ks = (kk * jnp.exp(jnp.minimum(b_mid - b, HG_CLAMP))).astype(BF16)
        lhs = jnp.concatenate([jnp.where(lane_head == h, qs, 0.0) for h in range(nh)], axis=0).astype(BF16)
        a = _dot_nt(lhs, ks)
        t_sub = lax.broadcasted_iota(I32, (nh * HG_SUB, C), 0) & (HG_SUB - 1)
        t_tau = HG_SUB * i + ((HG_SUB - 1 - t_sub) if rev else t_sub)
        s_idx = lax.broadcasted_iota(I32, (nh * HG_SUB, C), 1)
        s_tau = (C - 1 - s_idx) if rev else s_idx
        a_rows.append(jnp.where(s_tau <= t_tau, a, 0.0).astype(BF16))
    av = _dot(jnp.concatenate(a_rows, axis=0), vb)
    pieces = [None] * nsub
    for i in range(nsub):
        acc = jnp.zeros((HG_SUB, HW), F32)
        for h in range(nh):
            r0 = (i * nh + h) * HG_SUB
            acc = acc + jnp.where(lane_head == h, av[r0:r0 + HG_SUB, :], 0.0)
        pieces[starts[i] // HG_SUB] = acc
    o = jnp.concatenate(pieces, axis=0) + _dot_nt(q_inter.astype(BF16), st.astype(BF16))

    rh = lax.broadcasted_iota(I32, (HW, HW), 0) >> LOG_HEAD_DIM
    ch = lax.broadcasted_iota(I32, (HW, HW), 1) >> LOG_HEAD_DIM
    upd = _dot_tn(vb, k_end.astype(BF16))
    st_new = st * jnp.exp(b_last) + jnp.where(rh == ch, upd, 0.0)
    return o, st_new


def _hgrn_kernel(*refs, has_init, ns):
    seqs = [refs[6 * k:6 * k + 6] for k in range(ns)]
    rest = refs[6 * ns:]
    if has_init:
        s0, of, ob, sout, st = rest
    else:
        of, ob, sout, st = rest
        s0 = None
    c = pl.program_id(1)

    @pl.when(c == 0)
    def _():
        if has_init:
            st[...] = s0[...]
        else:
            st[...] = jnp.zeros_like(st)

    for k, (qf, vf, lff, qb, vb_, lfb) in enumerate(seqs):
        o, s_new = _hgrn_chunk(qf[0], vf[0], lff[0], st[k, 0], rev=False)
        of[k] = o
        st[k, 0] = s_new
        o, s_new = _hgrn_chunk(qb[0], vb_[0], lfb[0], st[k, 1], rev=True)
        ob[k] = o
        st[k, 1] = s_new

    @pl.when(c == pl.num_programs(1) - 1)
    def _():
        sout[...] = st[...]


def _hgrn(hp, nseq, n, s0=None):
    R, NT, W6 = hp.shape
    HW = W6 // 6
    nc = n // HG_CHUNK
    latent = s0 is not None
    ns = next(k for k in (4, 2, 1) if nseq % k == 0) if latent else 1

    def in_spec(colblk, back, k):
        def index(g, c):
            cc = nc - 1 - c if back else c
            return (1 + ns * g + k, cc, colblk) if latent else (0, g * nc + cc, colblk)
        return pl.BlockSpec((1, HG_CHUNK, HW), index)

    def out_spec(back):
        def index(g, c):
            cc = nc - 1 - c if back else c
            return (g, cc, 0) if latent else (0, g * nc + cc, 0)
        return pl.BlockSpec((ns, HG_CHUNK, HW), index)

    in_specs, args = [], []
    for k in range(ns):
        in_specs += [in_spec(0, False, k), in_spec(1, False, k), in_spec(2, False, k),
                     in_spec(0, True, k), in_spec(1, True, k), in_spec(3, True, k)]
        args += [hp] * 6
    state_spec = pl.BlockSpec((ns, 2, HW, HW), lambda g, c: (g, 0, 0, 0))
    if latent:
        in_specs.append(state_spec)
        args.append(s0)
    nout = nseq if latent else 1
    return pl.pallas_call(
        functools.partial(_hgrn_kernel, has_init=latent, ns=ns),
        grid=(nseq // ns, nc),
        in_specs=in_specs,
        out_specs=[out_spec(False), out_spec(True), state_spec],
        out_shape=[
            jax.ShapeDtypeStruct((nout, NT, HW), F32),
            jax.ShapeDtypeStruct((nout, NT, HW), F32),
            jax.ShapeDtypeStruct((nseq, 2, HW, HW), F32),
        ],
        scratch_shapes=[pltpu.VMEM((ns, 2, HW, HW), F32)],
        compiler_params=_cparams(("parallel", "arbitrary")),
        name="hgrn_scan",
    )(*args)


def _pool_kernel(a_ref, w_ref, sc_ref, o_ref):
    a = a_ref[0]
    n, W = a.shape
    t = lax.broadcasted_iota(I32, (n, 1), 0)

    def shift(x, d):
        src = t - d
        return jnp.where((src >= 0) & (src < n), pltpu.roll(x, d % n, axis=0), 0.0)

    trail = [a]
    fwd = [a]
    for j in range(len(POOL_HALF) - 1):
        h = POOL_HALF[j]
        trail.append(trail[j] + shift(trail[j], h))
        fwd.append(fwd[j] + shift(fwd[j], -h))
    grp = lax.broadcasted_iota(I32, (1, W), 1) >> LOG_HEAD_DIM
    win = jnp.zeros_like(a)
    half = jnp.zeros((1, W), I32)
    for j, h in enumerate(POOL_HALF):
        win = jnp.where(grp == j, shift(trail[j], 1) + fwd[j], win)
        half = jnp.where(grp == j, h, half)
    cnt = jnp.minimum(t + half, n) - jnp.maximum(t - half, 0)
    p = win / cnt.astype(F32) - a
    o_ref[0] = (_dot(p.astype(BF16), w_ref[...]) * sc_ref[...]).astype(BF16)


def _pool(hp, nseq, n, w_bd, scale, latent):
    R, NT, W6 = hp.shape
    W = W6 // 6
    return pl.pallas_call(
        _pool_kernel,
        grid=(nseq,),
        in_specs=[
            pl.BlockSpec((1, n, W), (lambda s: (s + 1, 0, 5)) if latent else (lambda s: (0, s, 5))),
            pl.BlockSpec((W, W), lambda s: (0, 0)),
            pl.BlockSpec((1, W), lambda s: (0, 0)),
        ],
        out_specs=pl.BlockSpec((1, n, W), (lambda s: (s, 0, 0)) if latent else (lambda s: (0, s, 0))),
        out_shape=jax.ShapeDtypeStruct((nseq if latent else 1, NT, W), BF16),
        compiler_params=_cparams(("parallel",), VMEM_LIMIT),
        name="pool_mixer",
    )(hp, w_bd, scale)


def _outproj_kernel(xa_ref, xb_ref, na_a, na_b, of_a, of_b, ob_a, ob_b, op_a, op_b, sg_ref,
                    wo_ref, hgn_ref, seg_ref, g1_ref, gpm_ref, gpf_ref, sc2_ref, sh2_ref, wrh_ref, wrl_ref,
                    x1_ref, u2_ref, lg_ref, *, na_w, hg_w):
    tm, d = x1_ref.shape[1:]
    nchunk = d // LANES
    nsub = 2
    sub = tm // nsub
    first = pl.program_id(0) == 0
    for h in range(nsub):
        rows = slice(h * sub, (h + 1) * sub)

        def pick(a_ref, b_ref):
            return jnp.where(first, a_ref[0, rows, :], b_ref[0, rows, :])

        o = pick(of_a, of_b) + pick(ob_a, ob_b)
        hi, lo = _split_bf16(o * o)
        ms = _dot(hi, seg_ref[...]) + _dot(lo, seg_ref[...])
        ohg = o * lax.rsqrt(ms + EPS) * hgn_ref[...] * sg_ref[0, rows, :]
        mix = (_dot(pick(na_a, na_b), wo_ref[0:na_w, :])
               + _dot(ohg.astype(BF16), wo_ref[na_w:na_w + hg_w, :])
               + _dot(pick(op_a, op_b), wo_ref[na_w + hg_w:, :]))
        x1 = pick(xa_ref, xb_ref) + g1_ref[0] * (_rms(mix) * gpm_ref[...])
        x1_ref[0, rows, :] = x1
        u2 = _rms(x1) * gpf_ref[...] * (1.0 + sc2_ref[0]) + sh2_ref[0]
        for c in range(nchunk):
            u2_ref[0, pl.ds(h * sub * nchunk + c, sub, stride=nchunk), :] = u2[:, c * LANES:(c + 1) * LANES]
        uh, ul = _split_bf16(u2)
        lg_ref[0, :, rows] = _dot_nt(wrh_ref[...], uh) + _dot_nt(wrl_ref[...], uh) + _dot_nt(wrh_ref[...], ul)


def _outproj(x, ona, of, ob, opool, hp, wo_bf, hg_norm, seg, g1, gpm, gpf, sc2, sh2, wr_hi, wr_lo):
    _, NT, D = x[0].shape
    R = x[1].shape[0] + 1
    na_w, hg_w = D // 2, D // 4
    E = wr_hi.shape[0]
    tm = TOK_TILE
    nb = NT // tm
    kern = functools.partial(_outproj_kernel, na_w=na_w, hg_w=hg_w)
    tok = lambda r, i: (r, i, 0)
    per_req = lambda r, i: (r, 0, 0)
    const2 = lambda r, i: (0, 0)
    pairs = (_stacked_x_specs(tm, D, nb) + _stacked_x_specs(tm, na_w, nb) + _stacked_x_specs(tm, hg_w, nb)
             + _stacked_x_specs(tm, hg_w, nb) + _stacked_x_specs(tm, hg_w, nb))
    return pl.pallas_call(
        kern,
        grid=(R, nb),
        in_specs=pairs + [
            pl.BlockSpec((1, tm, hg_w), lambda r, i: (r, i, 4)),
            pl.BlockSpec((D, D), const2),
            pl.BlockSpec((1, hg_w), const2),
            pl.BlockSpec((hg_w, hg_w), const2),
            pl.BlockSpec((1, 1, D), per_req),
            pl.BlockSpec((1, D), const2),
            pl.BlockSpec((1, D), const2),
            pl.BlockSpec((1, 1, D), per_req),
            pl.BlockSpec((1, 1, D), per_req),
            pl.BlockSpec((E, D), const2),
            pl.BlockSpec((E, D), const2),
        ],
        out_specs=[
            pl.BlockSpec((1, tm, D), tok),
            pl.BlockSpec((1, tm * (D // LANES), LANES), tok),
            pl.BlockSpec((1, E, tm), lambda r, i: (r, 0, i)),
        ],
        out_shape=[
            jax.ShapeDtypeStruct((R, NT, D), F32),
            jax.ShapeDtypeStruct((R, NT * (D // LANES), LANES), F32),
            jax.ShapeDtypeStruct((R, E, NT), F32),
        ],
        compiler_params=_cparams(("parallel", "parallel"), VMEM_LIMIT),
        name="outproj_router",
    )(*x, *ona, *of, *ob, *opool, hp, wo_bf, hg_norm, seg, g1, gpm, gpf, sc2, sh2, wr_hi, wr_lo)


def _prefix_lanes(mask_f, tri):
    rows, n = mask_f.shape
    T = tri.shape[0]
    outs = []
    carry = jnp.zeros((rows, 1), F32)
    for j in range(n // T):
        seg = mask_f[:, j * T:(j + 1) * T]
        pre = _dot(seg.astype(BF16), tri) + carry
        outs.append(pre)
        carry = pre[:, T - 1:T]
    return outs[0] if len(outs) == 1 else jnp.concatenate(outs, axis=1)


def _router_kernel(lg_ref, tri_ref, idx_ref, aff_ref, rank_sc, *, cap, nrow_groups):
    if nrow_groups == 1:
        lg = lg_ref[0]
        ex = jnp.exp(lg - jnp.max(lg, axis=0, keepdims=True))
        aff = ex / jnp.sum(ex, axis=0, keepdims=True)
    else:
        lg = lg_ref[...]
        ex = jnp.exp(lg - jnp.max(lg, axis=0, keepdims=True))
        aff = ex / jnp.sum(ex, axis=0, keepdims=True)
        aff = aff.reshape(lg.shape[0] * lg.shape[1], lg.shape[2])
    rows, n = aff.shape
    aff_ref[...] = aff.reshape(aff_ref.shape)
    thr = jnp.zeros((rows, 1), I32)
    for bit in range(30, -1, -1):
        cand = thr | (1 << bit)
        cnt = jnp.sum((aff >= pltpu.bitcast(cand, F32)).astype(F32), axis=-1, keepdims=True)
        thr = jnp.where(cnt >= cap, cand, thr)
    thr_f = pltpu.bitcast(thr, F32)
    gt = aff > thr_f
    eq = aff == thr_f
    need = cap - jnp.sum(gt.astype(F32), axis=-1, keepdims=True)
    tri = tri_ref[...]
    eq_f = eq.astype(F32)
    eq_before = _prefix_lanes(eq_f, tri) - eq_f
    sel = gt | (eq & (eq_before < need))
    rank_sc[...] = _prefix_lanes(sel.astype(F32), tri)

    ncol = idx_ref.shape[-1]
    jcol = lax.broadcasted_iota(I32, (cap, 1), 0).astype(F32)
    lane = lax.broadcasted_iota(I32, (1, ncol), 1)

    def body(r, acc):
        rk = rank_sc[pl.ds(r, 1), :]
        pos = jnp.sum((rk <= jcol).astype(F32), axis=-1, keepdims=True)
        return jnp.where(lane == r, pos, acc)

    idx = lax.fori_loop(0, rows, body, jnp.zeros((cap, ncol), F32))
    idx_ref[...] = idx.astype(I32).reshape(idx_ref.shape)


def _router_seg_kernel(lg_ref, tri_ref, mexc_ref, gsel_ref, idx_ref, aff_ref, loc_sc, pinc_sc, pexc_sc, *, cap):
    lg = lg_ref[0]
    E, S, _ = lg.shape
    rows = E * S
    log_s = S.bit_length() - 1
    ex = jnp.exp(lg - jnp.max(lg, axis=0, keepdims=True))
    aff = ex / jnp.sum(ex, axis=0, keepdims=True)
    aff_ref[0] = aff
    thr = jnp.zeros((E, 1, 1), I32)
    for bit in range(30, -1, -1):
        cand = thr | (1 << bit)
        cnt = jnp.sum((aff >= pltpu.bitcast(cand, F32)).astype(F32), axis=(1, 2), keepdims=True)
        thr = jnp.where(cnt >= cap, cand, thr)
    thr_f = pltpu.bitcast(thr, F32)
    gt = aff > thr_f
    eq = aff == thr_f
    need = cap - jnp.sum(gt.astype(F32), axis=(1, 2), keepdims=True)
    tri = tri_ref[...]
    eq2 = eq.astype(F32).reshape(rows, LANES)
    eq_loc = _dot(eq2.astype(BF16), tri)
    eq_tot = jnp.broadcast_to(eq_loc[:, LANES - 1:LANES], (rows, LANES)).astype(BF16)
    eq_before = (eq_loc - eq2 + _dot(mexc_ref[...], eq_tot)).reshape(E, S, LANES)
    sel = (gt | (eq & (eq_before < need))).astype(BF16).reshape(rows, LANES)
    loc_sc[0:rows, :] = _dot(sel, tri)
    loc_sc[rows:, :] = jnp.zeros((LANES, LANES), F32)
    tot_l = _dot_nt(jnp.ones((E, LANES), BF16), sel)
    own = (lax.broadcasted_iota(I32, (E, rows), 1) >> log_s) == lax.broadcasted_iota(I32, (E, rows), 0)
    tot_e = _dot(jnp.where(own, tot_l, 0.0).astype(BF16), gsel_ref[...])
    pinc = _dot(tot_e.astype(BF16), tri)
    pinc_sc[...] = pinc
    pexc_sc[...] = pinc - tot_e
    jcol = lax.broadcasted_iota(I32, (cap, 1), 0).astype(F32)
    lane = lax.broadcasted_iota(I32, (1, idx_ref.shape[-1]), 1)

    def body(e, acc):
        pinc_e = pinc_sc[pl.ds(e, 1), :]
        pexc_e = pexc_sc[pl.ds(e, 1), :]
        inside = (pexc_e <= jcol) & (jcol < pinc_e)
        nfull = jnp.sum((pinc_e <= jcol).astype(F32), axis=-1, keepdims=True)
        before = jnp.sum(jnp.where(inside, pexc_e, 0.0), axis=-1, keepdims=True)
        loc_e = loc_sc[pl.ds(pl.multiple_of(e * S, 8), LANES), :].astype(BF16)
        rsel = _dot(inside.astype(BF16), loc_e)
        cnt = jnp.sum((rsel <= jcol - before).astype(F32), axis=-1, keepdims=True)
        return jnp.where(lane == e, LANES * nfull + cnt, acc)

    idx = lax.fori_loop(0, E, body, jnp.zeros((cap, idx_ref.shape[-1]), F32))
    idx_ref[0] = idx.astype(I32)


def _router_seg(logits, cap):
    G, E, N = logits.shape
    S = N // LANES
    assert S & (S - 1) == 0
    rows = E * S
    tri = jnp.asarray(np.triu(np.ones((LANES, LANES), np.float32)), BF16)
    mexc = jnp.asarray(np.kron(np.eye(E), np.tril(np.ones((S, S)), -1)), BF16)
    gsel = jnp.asarray(np.tile(np.eye(S, LANES), (E, 1)), BF16)
    idx, aff = pl.pallas_call(
        functools.partial(_router_seg_kernel, cap=cap),
        grid=(G,),
        in_specs=[
            pl.BlockSpec((1, E, S, LANES), lambda g: (g, 0, 0, 0)),
            pl.BlockSpec((LANES, LANES), lambda g: (0, 0)),
            pl.BlockSpec((rows, rows), lambda g: (0, 0)),
            pl.BlockSpec((rows, LANES), lambda g: (0, 0)),
        ],
        out_specs=[pl.BlockSpec((1, cap, LANES), lambda g: (g, 0, 0)),
                   pl.BlockSpec((1, E, S, LANES), lambda g: (g, 0, 0, 0))],
        out_shape=[jax.ShapeDtypeStruct((G, cap, LANES), I32), jax.ShapeDtypeStruct((G, E, S, LANES), F32)],
        scratch_shapes=[pltpu.VMEM((rows + LANES, LANES), F32), pltpu.VMEM((E, LANES), F32),
                        pltpu.VMEM((E, LANES), F32)],
        compiler_params=_cparams(("parallel",), VMEM_LIMIT),
        name="router_topk_seg",
    )(logits.reshape(G, E, S, LANES), tri, mexc, gsel)
    return idx, aff.reshape(G, E, N)


def _router(logits, nq, n, cap, per_step_requests):
    G, E, NTOT = logits.shape
    rows = E * nq
    ncol = max(LANES, rows)
    T = min(n, 2 * LANES)
    tri = jnp.asarray(np.triu(np.ones((T, T), np.float32)), BF16)
    kern = functools.partial(_router_kernel, cap=cap, nrow_groups=nq)
    if nq == 1:
        lg_spec = pl.BlockSpec((1, E, n), lambda g: (g, 0, 0))
        lg_in = logits
        aff_spec = pl.BlockSpec((1, E, n), lambda g: (g, 0, 0))
        aff_shape = jax.ShapeDtypeStruct((G, E, n), F32)
    else:
        assert G == 1
        lg_in = logits.reshape(E, nq, n)
        lg_spec = pl.BlockSpec((E, nq, n), lambda g: (0, 0, 0))
        aff_spec = pl.BlockSpec((rows, n), lambda g: (0, 0))
        aff_shape = jax.ShapeDtypeStruct((rows, n), F32)
    return pl.pallas_call(
        kern,
        grid=(G,),
        in_specs=[lg_spec, pl.BlockSpec((T, T), lambda g: (0, 0))],
        out_specs=[pl.BlockSpec((1, cap, ncol), lambda g: (g, 0, 0)), aff_spec],
        out_shape=[jax.ShapeDtypeStruct((G, cap, ncol), I32), aff_shape],
        scratch_shapes=[pltpu.VMEM((rows, n), F32)],
        compiler_params=_cparams(("parallel",), VMEM_LIMIT),
        name="router_topk",
    )(lg_in, tri)


def _ffn_kernel(idx_ref, affp_ref, affc_ref, u_ref, wg_ref, wu_ref, wd_ref, y_hbm, xs, outs, y_acc, sem,
                *, cap, nchunk):
    r = pl.program_id(0)
    e = pl.program_id(1)
    ne = pl.num_programs(1)
    group = 8
    cur = e & 1
    oth = 1 - cur

    def tile(i):
        if isinstance(i, int):
            return pl.ds(i * nchunk, nchunk)
        return pl.ds(pl.multiple_of(i * nchunk, nchunk), nchunk)

    def gather_group(expert, buf, j0):
        for u in range(group):
            xs[buf, tile(j0 + u), :] = u_ref[0, tile(idx_ref[0, 0, expert * cap + j0 + u]), :]

    def scatter_group(expert, buf, gate_ref, scale, j0):
        ts = [idx_ref[0, 0, expert * cap + j0 + u] for u in range(group)]
        new = [y_acc[tile(t), :] + outs[buf, tile(j0 + u), :] * (gate_ref[0, 0, 0, t] * scale)
               for u, t in enumerate(ts)]
        for t, val in zip(ts, new):
            y_acc[tile(t), :] = val

    @pl.when(e == 0)
    def _():
        y_acc[...] = jnp.zeros_like(y_acc)
        outs[...] = jnp.zeros_like(outs)

        def body(jb, carry):
            gather_group(0, 0, jb * group)
            return carry

        lax.fori_loop(0, cap // group, body, 0)

    e_next = jnp.minimum(e + 1, ne - 1)
    e_prev = jnp.maximum(e - 1, 0)
    prev_scale = jnp.where(e > 0, 1.0, 0.0)
    xb = _load_token_tiles(xs.at[cur], cap, nchunk).astype(BF16)
    ff = wg_ref.shape[-1]
    nsplit = 4
    fb = ff // nsplit
    per = cap // nsplit
    o = None
    for n in range(nsplit):
        cols = slice(n * fb, (n + 1) * fb)
        hid = _silu(_dot(xb, wg_ref[0, 0, :, cols])) * _dot(xb, wu_ref[0, 0, :, cols])
        part = _dot(hid.astype(BF16), wd_ref[0, 0, cols, :])
        o = part if o is None else o + part
        for j0 in range(n * per, (n + 1) * per, group):
            gather_group(e_next, oth, j0)
            scatter_group(e_prev, oth, affp_ref, prev_scale, j0)
    for c in range(nchunk):
        outs[cur, pl.ds(c, cap, stride=nchunk), :] = o[:, c * LANES:(c + 1) * LANES]

    @pl.when(e == ne - 1)
    def _():
        def body(jb, carry):
            scatter_group(e, cur, affc_ref, 1.0, jb * group)
            return carry

        lax.fori_loop(0, cap // group, body, 0)
        cp = pltpu.make_async_copy(y_acc, y_hbm.at[r], sem)
        cp.start()
        cp.wait()


def _expert_ffn(u2, idx, aff, wg, wu, wd, layer):
    _, E, D, FF = wg.shape
    nchunk = D // LANES
    R, rows, _ = u2.shape
    NT = rows // nchunk
    cap = idx.shape[-1]
    return pl.pallas_call(
        functools.partial(_ffn_kernel, cap=cap, nchunk=nchunk),
        grid=(R, E),
        in_specs=[
            pl.BlockSpec((1, 1, E * cap), lambda r, e: (r, 0, 0), memory_space=pltpu.SMEM),
            pl.BlockSpec((1, 1, 1, NT), lambda r, e: (r, jnp.maximum(e - 1, 0), 0, 0), memory_space=pltpu.SMEM),
            pl.BlockSpec((1, 1, 1, NT), lambda r, e: (r, e, 0, 0), memory_space=pltpu.SMEM),
            pl.BlockSpec((1, rows, LANES), lambda r, e: (r, 0, 0), pipeline_mode=pl.Buffered(1)),
            pl.BlockSpec((1, 1, D, FF), lambda r, e: (layer, e, 0, 0)),
            pl.BlockSpec((1, 1, D, FF), lambda r, e: (layer, e, 0, 0)),
            pl.BlockSpec((1, 1, FF, D), lambda r, e: (layer, e, 0, 0)),
        ],
        out_specs=pl.BlockSpec(memory_space=pl.ANY),
        out_shape=jax.ShapeDtypeStruct((R, rows, LANES), F32),
        scratch_shapes=[
            pltpu.VMEM((2, cap * nchunk, LANES), F32),
            pltpu.VMEM((2, cap * nchunk, LANES), F32),
            pltpu.VMEM((rows, LANES), F32),
            pltpu.SemaphoreType.DMA(()),
        ],
        compiler_params=_cparams(("arbitrary", "arbitrary"), 60 * 1024 * 1024),
        name="expert_ffn",
    )(idx.reshape(R, 1, E * cap), aff.reshape(R, E, 1, NT), aff.reshape(R, E, 1, NT), u2, wg, wu, wd)


def _resid_kernel(x_ref, ff_ref, g2_ref, gn_ref, o_ref):
    n, d = x_ref.shape[1:]
    ff = _load_token_tiles(ff_ref.at[0], n, d // LANES)
    o_ref[0] = x_ref[0] + g2_ref[0] * (_rms(ff) * gn_ref[...])


def _ffn_residual(x1, ff, g2, gn, r0, nr):
    _, NT, D = x1.shape
    tm = TOK_TILE
    tok = lambda r, i: (r0 + r, i, 0)
    return pl.pallas_call(
        _resid_kernel,
        grid=(nr, NT // tm),
        in_specs=[
            pl.BlockSpec((1, tm, D), tok),
            pl.BlockSpec((1, tm * (D // LANES), LANES), tok),
            pl.BlockSpec((1, 1, D), lambda r, i: (r0 + r, 0, 0)),
            pl.BlockSpec((1, D), lambda r, i: (0, 0)),
        ],
        out_specs=pl.BlockSpec((1, tm, D), lambda r, i: (r, i, 0)),
        out_shape=jax.ShapeDtypeStruct((nr, NT, D), F32),
        compiler_params=_cparams(("parallel", "parallel")),
        name="ffn_residual",
    )(x1, ff, g2, gn)


def kernel(x_prompt, x_sample, cache_na_k, cache_na_v, state_hgrn, c, c_ctx, ada_w, ada_b,
           norm_pre_mix, norm_post_mix, norm_pre_ffn, norm_post_ffn, w_in, w_out, na_rpb,
           hg_lb, hg_norm, pool_w, pool_scale, w_router, w_gate, w_up, w_down):
    B, SEQ, D = x_prompt.shape
    NB, NT, _ = x_sample.shape
    L = ada_w.shape[0]
    assert B * SEQ == NT, "context tokens are stacked as one request of the latent sequence length"
    R = NB + 1
    na_w, hg_w = D // 2, D // 4
    H = na_w // HEAD_DIM
    nh = hg_w // HEAD_DIM
    E = w_router.shape[-1]
    past = cache_na_k.shape[2]
    cap_ctx = 2 * SEQ // E
    cap_lat = 2 * NT // E
    assert B * cap_ctx == cap_lat

    xa, xb = x_prompt.reshape(1, NT, D), x_sample
    RP = 16
    conds = jnp.zeros((RP, D), F32).at[0].set(c_ctx).at[1:R].set(c)
    mod = _modulation(conds, ada_w, ada_b)[:, :R].reshape(L, R, 6, 1, D)

    lb = jnp.cumsum(jax.nn.softmax(hg_lb.astype(F32), axis=0), axis=0)
    lb = lb - lb[:1]
    log_lb, log1m_lb = jnp.log(lb), jnp.log1p(-lb)

    ck = cache_na_k.reshape(NB, L, past, na_w)
    cv = cache_na_v.reshape(NB, L, past, na_w)
    seg = jnp.asarray(np.kron(np.eye(nh), np.full((HEAD_DIM, HEAD_DIM), 1.0 / HEAD_DIM)), BF16)
    eye_h = jnp.eye(nh, dtype=F32)
    s0_all = jnp.einsum('bldhkv,hg->bldhvgk', state_hgrn.astype(F32), eye_h).reshape(NB, L, 2, hg_w, hg_w)

    wg_bf, wu_bf, wd_bf = w_gate.astype(BF16), w_up.astype(BF16), w_down.astype(BF16)
    new_k, new_v, new_s = [], [], []
    for l in range(L):
        sh1, sc1, g1, sh2, sc2, g2 = (mod[l, :, j] for j in range(6))
        w_bf = w_in[l].astype(BF16)
        g0 = 3 * na_w + hg_w
        w_gate_lo = (w_in[l][:, g0:g0 + 2 * hg_w] - w_bf[:, g0:g0 + 2 * hg_w].astype(F32)).astype(BF16)
        q, k, v, kvf, hp = _inproj(xa, xb, norm_pre_mix[l][None], sc1, sh1, w_bf, w_gate_lo,
                                   log_lb[l], log1m_lb[l])
        new_k.append(kvf[:NT, :na_w].reshape(B, SEQ, H, HEAD_DIM))
        new_v.append(kvf[:NT, na_w:].reshape(B, SEQ, H, HEAD_DIM))

        tab = _na_bias_table(na_rpb[l])
        ona = (_ctx_attention(q, k, v, B, SEQ), _na_attention(q, k, v, ck, cv, l, tab))

        of_c, ob_c, st_c = _hgrn(hp, B, SEQ)
        of_l, ob_l, _ = _hgrn(hp, NB, NT, s0=s0_all[:, l])
        st_c = st_c.reshape(B, 2, nh, HEAD_DIM, nh, HEAD_DIM)
        new_s.append(jnp.einsum('bdhvgk,hg->bdhkv', st_c, eye_h))

        w_bd = jnp.einsum('gcd,gh->gchd', pool_w[l].astype(F32), jnp.eye(len(POOL_HALF), dtype=F32))
        w_bd = w_bd.reshape(hg_w, hg_w).astype(BF16)
        opool = (_pool(hp, B, SEQ, w_bd, pool_scale[l][None], False),
                 _pool(hp, NB, NT, w_bd, pool_scale[l][None], True))

        wr_hi, wr_lo = _split_bf16(w_router[l].T.astype(F32))
        x1, u2, logits = _outproj((xa, xb), ona, (of_c, of_l), (ob_c, ob_l), opool, hp, w_out[l].astype(BF16),
                                  hg_norm[l][None], seg, g1, norm_post_mix[l][None], norm_pre_ffn[l][None],
                                  sc2, sh2, wr_hi, wr_lo)

        idx_c, aff_c = _router(logits[:1], B, SEQ, cap_ctx, False)
        idx_l, aff_l = _router_seg(logits[1:], cap_lat)
        idx_c = idx_c[0, :, :E * B].reshape(cap_ctx, E, B).transpose(1, 2, 0)
        idx_c = (idx_c + (jnp.arange(B, dtype=I32) * SEQ)[None, :, None]).reshape(1, E, cap_lat)
        idx = jnp.concatenate([idx_c, idx_l[:, :, :E].transpose(0, 2, 1)], axis=0)
        aff = jnp.concatenate([aff_c.reshape(1, E, NT), aff_l], axis=0)

        ff = _expert_ffn(u2, idx, aff, wg_bf, wu_bf, wd_bf, l)
        xa = _ffn_residual(x1, ff, g2, norm_post_ffn[l][None], 0, 1)
        xb = _ffn_residual(x1, ff, g2, norm_post_ffn[l][None], 1, NB)

    y_prompt = xa.reshape(B, SEQ, D)
    y_sample = xb
    new_cache_k = jnp.stack(new_k, axis=1)
    new_cache_v = jnp.stack(new_v, axis=1)
    new_state = jnp.stack(new_s, axis=1).astype(x_prompt.dtype)
    return (y_prompt, y_sample, new_cache_k, new_cache_v, new_state)
```

```python
import functools

import numpy as np
import jax
import jax.numpy as jnp
from jax import lax
from jax.experimental import pallas as pl
from jax.experimental.pallas import tpu as pltpu

F32 = jnp.float32
BF16 = jnp.bfloat16
I32 = jnp.int32

HEAD_DIM = 64
LOG_HEAD_DIM = 6
LANES = 128
GRID_W = 64
NA_KH = 8
NA_KW = 16
NA_RB = 8
NA_WR = 16
HG_CHUNK = 64
HG_SUB = 16
HG_CLAMP = 80.0
POOL_HALF = (1, 2, 4, 8)
N_EXPERTS = 16
EPS = 1e-6
NEG_BIG = -1e30
TOK_TILE = 512
VMEM_LIMIT = 56 * 1024 * 1024


def _cparams(sem, vmem=None):
    return pltpu.CompilerParams(dimension_semantics=sem, vmem_limit_bytes=vmem)


def _silu(x):
    return x / (1.0 + jnp.exp(-x))


def _dot(a, b):
    return jnp.dot(a, b, preferred_element_type=F32)


def _dot_nt(a, b):
    return lax.dot_general(a, b, (((1,), (1,)), ((), ())), preferred_element_type=F32)


def _dot_tn(a, b):
    return lax.dot_general(a, b, (((0,), (0,)), ((), ())), preferred_element_type=F32)


def _rms(x):
    return x * lax.rsqrt(jnp.mean(x * x, axis=-1, keepdims=True) + EPS)


def _store_token_tiles(ref, x):
    n, d = x.shape
    nchunk = d // LANES
    for c in range(nchunk):
        ref[0, pl.ds(c, n, stride=nchunk), :] = x[:, c * LANES:(c + 1) * LANES]


def _load_token_tiles(ref2d, n, nchunk):
    return jnp.concatenate([ref2d[pl.ds(c, n, stride=nchunk), :] for c in range(nchunk)], axis=1)


def _split_bf16(x):
    hi = x.astype(BF16)
    lo = (x - hi.astype(F32)).astype(BF16)
    return hi, lo


def _mod_kernel(c_ref, w_ref, b_ref, o_ref):
    ah, al = _split_bf16(_silu(c_ref[...]))
    wh, wl = _split_bf16(w_ref[0])
    o_ref[0] = _dot(ah, wh) + _dot(al, wh) + _dot(ah, wl) + b_ref[0]


def _modulation(conds, ada_w, ada_b):
    L, D, D6 = ada_w.shape
    RP = conds.shape[0]
    return pl.pallas_call(
        _mod_kernel,
        grid=(L, D6 // D),
        in_specs=[
            pl.BlockSpec((RP, D), lambda l, j: (0, 0)),
            pl.BlockSpec((1, D, D), lambda l, j: (l, 0, j)),
            pl.BlockSpec((1, 1, D), lambda l, j: (l, 0, j)),
        ],
        out_specs=pl.BlockSpec((1, RP, D), lambda l, j: (l, 0, j)),
        out_shape=jax.ShapeDtypeStruct((L, RP, D6), F32),
        compiler_params=_cparams(("parallel", "parallel")),
        name="modulation",
    )(conds, ada_w, ada_b.reshape(L, 1, D6))


def _log_gate(x, la, l1):
    y = l1 + jnp.minimum(x, 0.0) - jnp.log1p(jnp.exp(-jnp.abs(x)))
    return jnp.maximum(la, y) + jnp.log1p(jnp.exp(-jnp.abs(la - y)))


def _stacked_x(xa_ref, xb_ref):
    return jnp.where(pl.program_id(0) == 0, xa_ref[0], xb_ref[0])


def _stacked_x_specs(tm, D, nb):
    return [
        pl.BlockSpec((1, tm, D), lambda r, i: (0, jnp.where(r == 0, i, nb - 1), 0)),
        pl.BlockSpec((1, tm, D), lambda r, i: (jnp.maximum(r - 1, 0), jnp.where(r == 0, 0, i), 0)),
    ]


def _inproj_kernel(xa_ref, xb_ref, g_ref, sc_ref, sh_ref, w_ref, wlo_ref, la_ref, l1_ref,
                   q_ref, k_ref, v_ref, kvf_ref, hp_ref, *, na_w, hg_w):
    u = _rms(_stacked_x(xa_ref, xb_ref)) * g_ref[...] * (1.0 + sc_ref[0]) + sh_ref[0]
    uh, ul = _split_bf16(u)
    p = _dot(uh, w_ref[...])
    g0 = 3 * na_w + hg_w
    gates = p[:, g0:g0 + 2 * hg_w] + _dot(ul, w_ref[:, g0:g0 + 2 * hg_w]) + _dot(uh, wlo_ref[...])
    k = p[:, na_w:2 * na_w]
    v = p[:, 2 * na_w:3 * na_w]
    q_ref[0] = (p[:, :na_w] * (HEAD_DIM ** -0.5)).astype(BF16)
    k_ref[0] = k.astype(BF16)
    v_ref[0] = v.astype(BF16)
    kvf_ref[:, :na_w] = k
    kvf_ref[:, na_w:] = v
    o = 3 * na_w

    def col(j):
        return p[:, o + j * hg_w:o + (j + 1) * hg_w]

    hp_ref[0, :, 0 * hg_w:1 * hg_w] = _silu(col(0))
    hp_ref[0, :, 1 * hg_w:2 * hg_w] = col(3)
    hp_ref[0, :, 2 * hg_w:3 * hg_w] = _log_gate(gates[:, :hg_w], la_ref[0:1, :], l1_ref[0:1, :])
    hp_ref[0, :, 3 * hg_w:4 * hg_w] = _log_gate(gates[:, hg_w:], la_ref[1:2, :], l1_ref[1:2, :])
    hp_ref[0, :, 4 * hg_w:5 * hg_w] = _silu(col(4))
    hp_ref[0, :, 5 * hg_w:6 * hg_w] = col(5)


def _inproj(xa, xb, g, sc, sh, w_in_bf, w_gate_lo, log_lb, log1m_lb):
    _, NT, D = xa.shape
    R = xb.shape[0] + 1
    na_w, hg_w = D // 2, D // 4
    inw = w_in_bf.shape[1]
    tm = TOK_TILE
    nb = NT // tm
    kern = functools.partial(_inproj_kernel, na_w=na_w, hg_w=hg_w)
    tok = lambda r, i: (r, i, 0)
    per_req = lambda r, i: (r, 0, 0)
    const2 = lambda r, i: (0, 0)
    return pl.pallas_call(
        kern,
        grid=(R, nb),
        in_specs=_stacked_x_specs(tm, D, nb) + [
            pl.BlockSpec((1, D), const2),
            pl.BlockSpec((1, 1, D), per_req),
            pl.BlockSpec((1, 1, D), per_req),
            pl.BlockSpec((D, inw), const2),
            pl.BlockSpec((D, 2 * hg_w), const2),
            pl.BlockSpec((2, hg_w), const2),
            pl.BlockSpec((2, hg_w), const2),
        ],
        out_specs=[
            pl.BlockSpec((1, tm, na_w), tok),
            pl.BlockSpec((1, tm, na_w), tok),
            pl.BlockSpec((1, tm, na_w), tok),
            pl.BlockSpec((tm, 2 * na_w), lambda r, i: (jnp.where(r == 0, i, nb), 0)),
            pl.BlockSpec((1, tm, 6 * hg_w), tok),
        ],
        out_shape=[
            jax.ShapeDtypeStruct((R, NT, na_w), BF16),
            jax.ShapeDtypeStruct((R, NT, na_w), BF16),
            jax.ShapeDtypeStruct((R, NT, na_w), BF16),
            jax.ShapeDtypeStruct(((nb + 1) * tm, 2 * na_w), F32),
            jax.ShapeDtypeStruct((R, NT, 6 * hg_w), F32),
        ],
        compiler_params=_cparams(("arbitrary", "arbitrary"), VMEM_LIMIT),
        name="prenorm_inproj",
    )(xa, xb, g, sc, sh, w_in_bf, w_gate_lo, log_lb, log1m_lb)


def _pair_stack(q):
    first = lax.broadcasted_iota(I32, (1, LANES), 1) < HEAD_DIM
    zero = jnp.zeros_like(q)
    return jnp.concatenate([jnp.where(first, q, zero), jnp.where(first, zero, q)], axis=0), first


def _ctx_attn_kernel(q_ref, k_ref, v_ref, o_ref):
    n = q_ref.shape[1]
    q2, first = _pair_stack(q_ref[0])
    s = _dot_nt(q2, k_ref[0])
    p = jnp.exp(s - jnp.max(s, axis=-1, keepdims=True))
    l = jnp.sum(p, axis=-1, keepdims=True)
    o2 = _dot(p.astype(BF16), v_ref[0]) * (1.0 / l)
    o_ref[0] = jnp.where(first, o2[:n], o2[n:]).astype(BF16)


def _ctx_attention(q, k, v, nseq, seq):
    W = q.shape[-1]
    blk = pl.BlockSpec((1, seq, LANES), lambda s, p: (0, s, p))
    return pl.pallas_call(
        _ctx_attn_kernel,
        grid=(nseq, W // LANES),
        in_specs=[blk, blk, blk],
        out_specs=pl.BlockSpec((1, seq, LANES), lambda s, p: (0, s, p)),
        out_shape=jax.ShapeDtypeStruct((1, nseq * seq, W), BF16),
        compiler_params=_cparams(("parallel", "parallel")),
        name="ctx_attention",
    )(q, k, v)


def _na_bias_table(rpb):
    L, H = rpb.shape[:2]
    c = np.arange(GRID_W)[:, None]
    ck = np.arange(GRID_W)[None, :]
    wsc = np.clip(c - NA_KW // 2, 0, GRID_W - NA_KW)
    cvalid = (ck >= wsc) & (ck < wsc + NA_KW)
    nd, nx = 2 * NA_KH - 1, 2 * NA_KW - 1
    xsel = np.where(cvalid, np.clip(ck - c + NA_KW - 1, 0, nx - 1), nx)
    col_onehot = jnp.asarray(np.eye(nx + 1, dtype=np.float32)[xsel])
    row_onehot = np.zeros((NA_KH, NA_KH, nd), np.float32)
    for d in range(NA_KH):
        row_onehot[d, np.arange(NA_KH), d + np.arange(NA_KH)] = 1.0
    rpb_ext = jnp.concatenate([rpb.astype(F32), jnp.full((L, H, nd, 1), NEG_BIG, F32)], axis=3)
    ts = jnp.einsum('lhax,dja,ckx->lhdcjk', rpb_ext, jnp.asarray(row_onehot), col_onehot,
                    precision=lax.Precision.HIGHEST)
    return ts.reshape(L, H, NA_KH, GRID_W, NA_KH * GRID_W)


def _na_attn_kernel(q_ref, k_ref, v_ref, kc_ref, vc_ref, ts_ref, o_ref, pc_sc, *, rows):
    i = pl.program_id(2)
    nq = NA_RB * GRID_W
    nkl = NA_KH * GRID_W
    q2, first = _pair_stack(q_ref[0])
    kc = kc_ref[0, 0].astype(BF16)
    vc = vc_ref[0, 0].astype(BF16)
    s_c = _dot_nt(q2, kc)
    o_loc, l_inv = [], []
    for qr in range(NA_RB):
        r = NA_RB * i + qr
        rs = jnp.clip(r - NA_KH // 2, 0, rows - NA_KH)
        d0 = rs - r + NA_KH - 1
        start = pl.multiple_of(rs * GRID_W, GRID_W)
        a, b = qr * GRID_W, nq + qr * GRID_W
        q2r = jnp.concatenate([q2[a:a + GRID_W], q2[b:b + GRID_W]], axis=0)
        bias = jnp.concatenate([ts_ref[0, 0, d0], ts_ref[0, 1, d0]], axis=0)
        s_l = _dot_nt(q2r, k_ref[0, pl.ds(start, nkl), :]) + bias
        s_cr = jnp.concatenate([s_c[a:a + GRID_W], s_c[b:b + GRID_W]], axis=0)
        m = jnp.maximum(jnp.max(s_l, axis=-1, keepdims=True), jnp.max(s_cr, axis=-1, keepdims=True))
        p_l = jnp.exp(s_l - m)
        p_c = jnp.exp(s_cr - m)
        l_inv.append(1.0 / (jnp.sum(p_l, axis=-1, keepdims=True) + jnp.sum(p_c, axis=-1, keepdims=True)))
        o_loc.append(_dot(p_l.astype(BF16), v_ref[0, pl.ds(start, nkl), :]))
        pc_sc[2 * a:2 * a + 2 * GRID_W, :] = p_c.astype(BF16)
    o_ctx = _dot(pc_sc[...], vc)
    for qr in range(NA_RB):
        a = qr * GRID_W
        o2 = (o_loc[qr] + o_ctx[2 * a:2 * a + 2 * GRID_W]) * l_inv[qr]
        o_ref[0, a:a + GRID_W, :] = jnp.where(first, o2[:GRID_W], o2[GRID_W:]).astype(BF16)


def _na_attention(q, k, v, cache_k, cache_v, layer, tab):
    R, NT, W = q.shape
    nreq = R - 1
    rows = NT // GRID_W
    assert rows % NA_RB == 0 and rows >= NA_KH
    nb = rows // NA_RB
    past = cache_k.shape[2]
    nq = NA_RB * GRID_W
    kern = functools.partial(_na_attn_kernel, rows=rows)
    slab = pl.BlockSpec((1, NT, LANES), lambda b, p, i: (b + 1, 0, p))
    ctx = pl.BlockSpec((1, 1, past, LANES), lambda b, p, i: (b, layer, 0, p))
    return pl.pallas_call(
        kern,
        grid=(nreq, W // LANES, nb),
        in_specs=[
            pl.BlockSpec((1, nq, LANES), lambda b, p, i: (b + 1, i, p)),
            slab, slab, ctx, ctx,
            pl.BlockSpec((1, 2, NA_KH, GRID_W, NA_KH * GRID_W), lambda b, p, i: (layer, p, 0, 0, 0)),
        ],
        out_specs=pl.BlockSpec((1, nq, LANES), lambda b, p, i: (b, i, p)),
        out_shape=jax.ShapeDtypeStruct((nreq, NT, W), BF16),
        scratch_shapes=[pltpu.VMEM((2 * nq, past), BF16)],
        compiler_params=_cparams(("parallel", "parallel", "arbitrary"), VMEM_LIMIT),
        name="na_attention",
    )(q, k, v, cache_k, cache_v, tab)


def _hgrn_chunk(q, v, lf, st, rev):
    C, HW = q.shape
    nh = HW // HEAD_DIM
    nsub = C // HG_SUB
    row = lax.broadcasted_iota(I32, (C, 1), 0)
    tau = (C - 1 - row) if rev else row
    b = lf
    d = 1
    while d < C:
        shifted = pltpu.roll(b, (C - d) if rev else d, axis=0)
        b = b + jnp.where(tau >= d, shifted, 0.0)
        d *= 2
    kk = 1.0 - jnp.exp(lf)
    last = 0 if rev else C - 1
    b_last = b[last:last + 1, :]
    q_inter = q * jnp.exp(b)
    k_end = kk * jnp.exp(b_last - b)

    lane_head = lax.broadcasted_iota(I32, (1, HW), 1) >> LOG_HEAD_DIM
    vb = v.astype(BF16)
    a_rows = []
    starts = []
    for i in range(nsub):
        mid_tau = HG_SUB * i + HG_SUB // 2 - 1
        mid = (C - 1 - mid_tau) if rev else mid_tau
        b_mid = b[mid:mid + 1, :]
        ps = (C - HG_SUB * (i + 1)) if rev else HG_SUB * i
        starts.append(ps)
        qs = q[ps:ps + HG_SUB, :] * jnp.exp(jnp.minimum(b[ps:ps + HG_SUB, :] - b_mid, HG_CLAMP))
        ks = (kk * jnp.exp(jnp.minimum(b_mid - b, HG_CLAMP))).astype(BF16)
        lhs = jnp.concatenate([jnp.where(lane_head == h, qs, 0.0) for h in range(nh)], axis=0).astype(BF16)
        a = _dot_nt(lhs, ks)
        t_sub = lax.broadcasted_iota(I32, (nh * HG_SUB, C), 0) & (HG_SUB - 1)
        t_tau = HG_SUB * i + ((HG_SUB - 1 - t_sub) if rev else t_sub)
        s_idx = lax.broadcasted_iota(I32, (nh * HG_SUB, C), 1)
        s_tau = (C - 1 - s_idx) if rev else s_idx
        a_rows.append(jnp.where(s_tau <= t_tau, a, 0.0).astype(BF16))
    av = _dot(jnp.concatenate(a_rows, axis=0), vb)
    pieces = [None] * nsub
    for i in range(nsub):
        acc = jnp.zeros((HG_SUB, HW), F32)
        for h in range(nh):
            r0 = (i * nh + h) * HG_SUB
            acc = acc + jnp.where(lane_head == h, av[r0:r0 + HG_SUB, :], 0.0)
        pieces[starts[i] // HG_SUB] = acc
    o = jnp.concatenate(pieces, axis=0) + _dot_nt(q_inter.astype(BF16), st.astype(BF16))

    rh = lax.broadcasted_iota(I32, (HW, HW), 0) >> LOG_HEAD_DIM
    ch = lax.broadcasted_iota(I32, (HW, HW), 1) >> LOG_HEAD_DIM
    upd = _dot_tn(vb, k_end.astype(BF16))
    st_new = st * jnp.exp(b_last) + jnp.where(rh == ch, upd, 0.0)
    return o, st_new


def _hgrn_kernel(*refs, has_init, ns):
    seqs = [refs[6 * k:6 * k + 6] for k in range(ns)]
    rest = refs[6 * ns:]
    if has_init:
        s0, of, ob, sout, st = rest
    else:
        of, ob, sout, st = rest
        s0 = None
    c = pl.program_id(1)

    @pl.when(c == 0)
    def _():
        if has_init:
            st[...] = s0[...]
        else:
            st[...] = jnp.zeros_like(st)

    for k, (qf, vf, lff, qb, vb_, lfb) in enumerate(seqs):
        o, s_new = _hgrn_chunk(qf[0], vf[0], lff[0], st[k, 0], rev=False)
        of[k] = o
        st[k, 0] = s_new
        o, s_new = _hgrn_chunk(qb[0], vb_[0], lfb[0], st[k, 1], rev=True)
        ob[k] = o
        st[k, 1] = s_new

    @pl.when(c == pl.num_programs(1) - 1)
    def _():
        sout[...] = st[...]


def _hgrn(hp, nseq, n, s0=None):
    R, NT, W6 = hp.shape
    HW = W6 // 6
    nc = n // HG_CHUNK
    latent = s0 is not None
    ns = next(k for k in (4, 2, 1) if nseq % k == 0) if latent else 1

    def in_spec(colblk, back, k):
        def index(g, c):
            cc = nc - 1 - c if back else c
            return (1 + ns * g + k, cc, colblk) if latent else (0, g * nc + cc, colblk)
        return pl.BlockSpec((1, HG_CHUNK, HW), index)

    def out_spec(back):
        def index(g, c):
            cc = nc - 1 - c if back else c
            return (g, cc, 0) if latent else (0, g * nc + cc, 0)
        return pl.BlockSpec((ns, HG_CHUNK, HW), index)

    in_specs, args = [], []
    for k in range(ns):
        in_specs += [in_spec(0, False, k), in_spec(1, False, k), in_spec(2, False, k),
                     in_spec(0, True, k), in_spec(1, True, k), in_spec(3, True, k)]
        args += [hp] * 6
    state_spec = pl.BlockSpec((ns, 2, HW, HW), lambda g, c: (g, 0, 0, 0))
    if latent:
        in_specs.append(state_spec)
        args.append(s0)
    nout = nseq if latent else 1
    return pl.pallas_call(
        functools.partial(_hgrn_kernel, has_init=latent, ns=ns),
        grid=(nseq // ns, nc),
        in_specs=in_specs,
        out_specs=[out_spec(False), out_spec(True), state_spec],
        out_shape=[
            jax.ShapeDtypeStruct((nout, NT, HW), F32),
            jax.ShapeDtypeStruct((nout, NT, HW), F32),
            jax.ShapeDtypeStruct((nseq, 2, HW, HW), F32),
        ],
        scratch_shapes=[pltpu.VMEM((ns, 2, HW, HW), F32)],
        compiler_params=_cparams(("parallel", "arbitrary")),
        name="hgrn_scan",
    )(*args)


def _pool_kernel(a_ref, w_ref, sc_ref, o_ref):
    a = a_ref[0]
    n, W = a.shape
    t = lax.broadcasted_iota(I32, (n, 1), 0)

    def shift(x, d):
        src = t - d
        return jnp.where((src >= 0) & (src < n), pltpu.roll(x, d % n, axis=0), 0.0)

    trail = [a]
    fwd = [a]
    for j in range(len(POOL_HALF) - 1):
        h = POOL_HALF[j]
        trail.append(trail[j] + shift(trail[j], h))
        fwd.append(fwd[j] + shift(fwd[j], -h))
    grp = lax.broadcasted_iota(I32, (1, W), 1) >> LOG_HEAD_DIM
    win = jnp.zeros_like(a)
    half = jnp.zeros((1, W), I32)
    for j, h in enumerate(POOL_HALF):
        win = jnp.where(grp == j, shift(trail[j], 1) + fwd[j], win)
        half = jnp.where(grp == j, h, half)
    cnt = jnp.minimum(t + half, n) - jnp.maximum(t - half, 0)
    p = win / cnt.astype(F32) - a
    o_ref[0] = (_dot(p.astype(BF16), w_ref[...]) * sc_ref[...]).astype(BF16)


def _pool(hp, nseq, n, w_bd, scale, latent):
    R, NT, W6 = hp.shape
    W = W6 // 6
    return pl.pallas_call(
        _pool_kernel,
        grid=(nseq,),
        in_specs=[
            pl.BlockSpec((1, n, W), (lambda s: (s + 1, 0, 5)) if latent else (lambda s: (0, s, 5))),
            pl.BlockSpec((W, W), lambda s: (0, 0)),
            pl.BlockSpec((1, W), lambda s: (0, 0)),
        ],
        out_specs=pl.BlockSpec((1, n, W), (lambda s: (s, 0, 0)) if latent else (lambda s: (0, s, 0))),
        out_shape=jax.ShapeDtypeStruct((nseq if latent else 1, NT, W), BF16),
        compiler_params=_cparams(("parallel",), VMEM_LIMIT),
        name="pool_mixer",
    )(hp, w_bd, scale)


def _outproj_kernel(xa_ref, xb_ref, na_a, na_b, of_a, of_b, ob_a, ob_b, op_a, op_b, sg_ref,
                    wo_ref, hgn_ref, seg_ref, g1_ref, gpm_ref, gpf_ref, sc2_ref, sh2_ref, wrh_ref, wrl_ref,
                    x1_ref, u2_ref, lg_ref, *, na_w, hg_w):
    tm, d = x1_ref.shape[1:]
    nchunk = d // LANES
    nsub = 2
    sub = tm // nsub
    first = pl.program_id(0) == 0
    for h in range(nsub):
        rows = slice(h * sub, (h + 1) * sub)

        def pick(a_ref, b_ref):
            return jnp.where(first, a_ref[0, rows, :], b_ref[0, rows, :])

        o = pick(of_a, of_b) + pick(ob_a, ob_b)
        hi, lo = _split_bf16(o * o)
        ms = _dot(hi, seg_ref[...]) + _dot(lo, seg_ref[...])
        ohg = o * lax.rsqrt(ms + EPS) * hgn_ref[...] * sg_ref[0, rows, :]
        mix = (_dot(pick(na_a, na_b), wo_ref[0:na_w, :])
               + _dot(ohg.astype(BF16), wo_ref[na_w:na_w + hg_w, :])
               + _dot(pick(op_a, op_b), wo_ref[na_w + hg_w:, :]))
        x1 = pick(xa_ref, xb_ref) + g1_ref[0] * (_rms(mix) * gpm_ref[...])
        x1_ref[0, rows, :] = x1
        u2 = _rms(x1) * gpf_ref[...] * (1.0 + sc2_ref[0]) + sh2_ref[0]
        for c in range(nchunk):
            u2_ref[0, pl.ds(h * sub * nchunk + c, sub, stride=nchunk), :] = u2[:, c * LANES:(c + 1) * LANES]
        uh, ul = _split_bf16(u2)
        lg_ref[0, :, rows] = _dot_nt(wrh_ref[...], uh) + _dot_nt(wrl_ref[...], uh) + _dot_nt(wrh_ref[...], ul)


def _outproj(x, ona, of, ob, opool, hp, wo_bf, hg_norm, seg, g1, gpm, gpf, sc2, sh2, wr_hi, wr_lo):
    _, NT, D = x[0].shape
    R = x[1].shape[0] + 1
    na_w, hg_w = D // 2, D // 4
    E = wr_hi.shape[0]
    tm = TOK_TILE
    nb = NT // tm
    kern = functools.partial(_outproj_kernel, na_w=na_w, hg_w=hg_w)
    tok = lambda r, i: (r, i, 0)
    per_req = lambda r, i: (r, 0, 0)
    const2 = lambda r, i: (0, 0)
    pairs = (_stacked_x_specs(tm, D, nb) + _stacked_x_specs(tm, na_w, nb) + _stacked_x_specs(tm, hg_w, nb)
             + _stacked_x_specs(tm, hg_w, nb) + _stacked_x_specs(tm, hg_w, nb))
    return pl.pallas_call(
        kern,
        grid=(R, nb),
        in_specs=pairs + [
            pl.BlockSpec((1, tm, hg_w), lambda r, i: (r, i, 4)),
            pl.BlockSpec((D, D), const2),
            pl.BlockSpec((1, hg_w), const2),
            pl.BlockSpec((hg_w, hg_w), const2),
            pl.BlockSpec((1, 1, D), per_req),
            pl.BlockSpec((1, D), const2),
            pl.BlockSpec((1, D), const2),
            pl.BlockSpec((1, 1, D), per_req),
            pl.BlockSpec((1, 1, D), per_req),
            pl.BlockSpec((E, D), const2),
            pl.BlockSpec((E, D), const2),
        ],
        out_specs=[
            pl.BlockSpec((1, tm, D), tok),
            pl.BlockSpec((1, tm * (D // LANES), LANES), tok),
            pl.BlockSpec((1, E, tm), lambda r, i: (r, 0, i)),
        ],
        out_shape=[
            jax.ShapeDtypeStruct((R, NT, D), F32),
            jax.ShapeDtypeStruct((R, NT * (D // LANES), LANES), F32),
            jax.ShapeDtypeStruct((R, E, NT), F32),
        ],
        compiler_params=_cparams(("parallel", "parallel"), VMEM_LIMIT),
        name="outproj_router",
    )(*x, *ona, *of, *ob, *opool, hp, wo_bf, hg_norm, seg, g1, gpm, gpf, sc2, sh2, wr_hi, wr_lo)


def _prefix_lanes(mask_f, tri):
    rows, n = mask_f.shape
    T = tri.shape[0]
    outs = []
    carry = jnp.zeros((rows, 1), F32)
    for j in range(n // T):
        seg = mask_f[:, j * T:(j + 1) * T]
        pre = _dot(seg.astype(BF16), tri) + carry
        outs.append(pre)
        carry = pre[:, T - 1:T]
    return outs[0] if len(outs) == 1 else jnp.concatenate(outs, axis=1)


def _router_kernel(lg_ref, tri_ref, idx_ref, aff_ref, rank_sc, *, cap, nrow_groups):
    if nrow_groups == 1:
        lg = lg_ref[0]
        ex = jnp.exp(lg - jnp.max(lg, axis=0, keepdims=True))
        aff = ex / jnp.sum(ex, axis=0, keepdims=True)
    else:
        lg = lg_ref[...]
        ex = jnp.exp(lg - jnp.max(lg, axis=0, keepdims=True))
        aff = ex / jnp.sum(ex, axis=0, keepdims=True)
        aff = aff.reshape(lg.shape[0] * lg.shape[1], lg.shape[2])
    rows, n = aff.shape
    aff_ref[...] = aff.reshape(aff_ref.shape)
    thr = jnp.zeros((rows, 1), I32)
    for bit in range(30, -1, -1):
        cand = thr | (1 << bit)
        cnt = jnp.sum((aff >= pltpu.bitcast(cand, F32)).astype(F32), axis=-1, keepdims=True)
        thr = jnp.where(cnt >= cap, cand, thr)
    thr_f = pltpu.bitcast(thr, F32)
    gt = aff > thr_f
    eq = aff == thr_f
    need = cap - jnp.sum(gt.astype(F32), axis=-1, keepdims=True)
    tri = tri_ref[...]
    eq_f = eq.astype(F32)
    eq_before = _prefix_lanes(eq_f, tri) - eq_f
    sel = gt | (eq & (eq_before < need))
    rank_sc[...] = _prefix_lanes(sel.astype(F32), tri)

    ncol = idx_ref.shape[-1]
    jcol = lax.broadcasted_iota(I32, (cap, 1), 0).astype(F32)
    lane = lax.broadcasted_iota(I32, (1, ncol), 1)

    def body(r, acc):
        rk = rank_sc[pl.ds(r, 1), :]
        pos = jnp.sum((rk <= jcol).astype(F32), axis=-1, keepdims=True)
        return jnp.where(lane == r, pos, acc)

    idx = lax.fori_loop(0, rows, body, jnp.zeros((cap, ncol), F32))
    idx_ref[...] = idx.astype(I32).reshape(idx_ref.shape)


def _router_seg_kernel(lg_ref, tri_ref, mexc_ref, gsel_ref, idx_ref, aff_ref, loc_sc, pinc_sc, pexc_sc, *, cap):
    lg = lg_ref[0]
    E, S, _ = lg.shape
    rows = E * S
    log_s = S.bit_length() - 1
    ex = jnp.exp(lg - jnp.max(lg, axis=0, keepdims=True))
    aff = ex / jnp.sum(ex, axis=0, keepdims=True)
    aff_ref[0] = aff
    thr = jnp.zeros((E, 1, 1), I32)
    for bit in range(30, -1, -1):
        cand = thr | (1 << bit)
        cnt = jnp.sum((aff >= pltpu.bitcast(cand, F32)).astype(F32), axis=(1, 2), keepdims=True)
        thr = jnp.where(cnt >= cap, cand, thr)
    thr_f = pltpu.bitcast(thr, F32)
    gt = aff > thr_f
    eq = aff == thr_f
    need = cap - jnp.sum(gt.astype(F32), axis=(1, 2), keepdims=True)
    tri = tri_ref[...]
    eq2 = eq.astype(F32).reshape(rows, LANES)
    eq_loc = _dot(eq2.astype(BF16), tri)
    eq_tot = jnp.broadcast_to(eq_loc[:, LANES - 1:LANES], (rows, LANES)).astype(BF16)
    eq_before = (eq_loc - eq2 + _dot(mexc_ref[...], eq_tot)).reshape(E, S, LANES)
    sel = (gt | (eq & (eq_before < need))).astype(BF16).reshape(rows, LANES)
    loc_sc[0:rows, :] = _dot(sel, tri)
    loc_sc[rows:, :] = jnp.zeros((LANES, LANES), F32)
    tot_l = _dot_nt(jnp.ones((E, LANES), BF16), sel)
    own = (lax.broadcasted_iota(I32, (E, rows), 1) >> log_s) == lax.broadcasted_iota(I32, (E, rows), 0)
    tot_e = _dot(jnp.where(own, tot_l, 0.0).astype(BF16), gsel_ref[...])
    pinc = _dot(tot_e.astype(BF16), tri)
    pinc_sc[...] = pinc
    pexc_sc[...] = pinc - tot_e
    jcol = lax.broadcasted_iota(I32, (cap, 1), 0).astype(F32)
    lane = lax.broadcasted_iota(I32, (1, idx_ref.shape[-1]), 1)

    def body(e, acc):
        pinc_e = pinc_sc[pl.ds(e, 1), :]
        pexc_e = pexc_sc[pl.ds(e, 1), :]
        inside = (pexc_e <= jcol) & (jcol < pinc_e)
        nfull = jnp.sum((pinc_e <= jcol).astype(F32), axis=-1, keepdims=True)
        before = jnp.sum(jnp.where(inside, pexc_e, 0.0), axis=-1, keepdims=True)
        loc_e = loc_sc[pl.ds(pl.multiple_of(e * S, 8), LANES), :].astype(BF16)
        rsel = _dot(inside.astype(BF16), loc_e)
        cnt = jnp.sum((rsel <= jcol - before).astype(F32), axis=-1, keepdims=True)
        return jnp.where(lane == e, LANES * nfull + cnt, acc)

    idx = lax.fori_loop(0, E, body, jnp.zeros((cap, idx_ref.shape[-1]), F32))
    idx_ref[0] = idx.astype(I32)


def _router_seg(logits, cap):
    G, E, N = logits.shape
    S = N // LANES
    assert S & (S - 1) == 0
    rows = E * S
    tri = jnp.asarray(np.triu(np.ones((LANES, LANES), np.float32)), BF16)
    mexc = jnp.asarray(np.kron(np.eye(E), np.tril(np.ones((S, S)), -1)), BF16)
    gsel = jnp.asarray(np.tile(np.eye(S, LANES), (E, 1)), BF16)
    idx, aff = pl.pallas_call(
        functools.partial(_router_seg_kernel, cap=cap),
        grid=(G,),
        in_specs=[
            pl.BlockSpec((1, E, S, LANES), lambda g: (g, 0, 0, 0)),
            pl.BlockSpec((LANES, LANES), lambda g: (0, 0)),
            pl.BlockSpec((rows, rows), lambda g: (0, 0)),
            pl.BlockSpec((rows, LANES), lambda g: (0, 0)),
        ],
        out_specs=[pl.BlockSpec((1, cap, LANES), lambda g: (g, 0, 0)),
                   pl.BlockSpec((1, E, S, LANES), lambda g: (g, 0, 0, 0))],
        out_shape=[jax.ShapeDtypeStruct((G, cap, LANES), I32), jax.ShapeDtypeStruct((G, E, S, LANES), F32)],
        scratch_shapes=[pltpu.VMEM((rows + LANES, LANES), F32), pltpu.VMEM((E, LANES), F32),
                        pltpu.VMEM((E, LANES), F32)],
        compiler_params=_cparams(("parallel",), VMEM_LIMIT),
        name="router_topk_seg",
    )(logits.reshape(G, E, S, LANES), tri, mexc, gsel)
    return idx, aff.reshape(G, E, N)


def _router(logits, nq, n, cap, per_step_requests):
    G, E, NTOT = logits.shape
    rows = E * nq
    ncol = max(LANES, rows)
    T = min(n, 2 * LANES)
    tri = jnp.asarray(np.triu(np.ones((T, T), np.float32)), BF16)
    kern = functools.partial(_router_kernel, cap=cap, nrow_groups=nq)
    if nq == 1:
        lg_spec = pl.BlockSpec((1, E, n), lambda g: (g, 0, 0))
        lg_in = logits
        aff_spec = pl.BlockSpec((1, E, n), lambda g: (g, 0, 0))
        aff_shape = jax.ShapeDtypeStruct((G, E, n), F32)
    else:
        assert G == 1
        lg_in = logits.reshape(E, nq, n)
        lg_spec = pl.BlockSpec((E, nq, n), lambda g: (0, 0, 0))
        aff_spec = pl.BlockSpec((rows, n), lambda g: (0, 0))
        aff_shape = jax.ShapeDtypeStruct((rows, n), F32)
    return pl.pallas_call(
        kern,
        grid=(G,),
        in_specs=[lg_spec, pl.BlockSpec((T, T), lambda g: (0, 0))],
        out_specs=[pl.BlockSpec((1, cap, ncol), lambda g: (g, 0, 0)), aff_spec],
        out_shape=[jax.ShapeDtypeStruct((G, cap, ncol), I32), aff_shape],
        scratch_shapes=[pltpu.VMEM((rows, n), F32)],
        compiler_params=_cparams(("parallel",), VMEM_LIMIT),
        name="router_topk",
    )(lg_in, tri)


def _ffn_kernel(idx_ref, affp_ref, affc_ref, u_ref, wg_ref, wu_ref, wd_ref, y_hbm, xs, outs, y_acc, sem,
                *, cap, nchunk):
    r = pl.program_id(0)
    e = pl.program_id(1)
    ne = pl.num_programs(1)
    group = 8
    cur = e & 1
    oth = 1 - cur

    def tile(i):
        if isinstance(i, int):
            return pl.ds(i * nchunk, nchunk)
        return pl.ds(pl.multiple_of(i * nchunk, nchunk), nchunk)

    def gather_group(expert, buf, j0):
        for u in range(group):
            xs[buf, tile(j0 + u), :] = u_ref[0, tile(idx_ref[0, 0, expert * cap + j0 + u]), :]

    def scatter_group(expert, buf, gate_ref, scale, j0):
        ts = [idx_ref[0, 0, expert * cap + j0 + u] for u in range(group)]
        new = [y_acc[tile(t), :] + outs[buf, tile(j0 + u), :] * (gate_ref[0, 0, 0, t] * scale)
               for u, t in enumerate(ts)]
        for t, val in zip(ts, new):
            y_acc[tile(t), :] = val

    def writeback(req):
        return pltpu.make_async_copy(y_acc, y_hbm.at[req], sem)

    @pl.when((e == 0) & (r > 0))
    def _():
        writeback(r - 1).wait()

    @pl.when(e == 0)
    def _():
        y_acc[...] = jnp.zeros_like(y_acc)
        outs[...] = jnp.zeros_like(outs)

        def body(jb, carry):
            gather_group(0, 0, jb * group)
            return carry

        lax.fori_loop(0, cap // group, body, 0)

    e_next = jnp.minimum(e + 1, ne - 1)
    e_prev = jnp.maximum(e - 1, 0)
    prev_scale = jnp.where(e > 0, 1.0, 0.0)
    xb = _load_token_tiles(xs.at[cur], cap, nchunk).astype(BF16)
    ff = wg_ref.shape[-1]
    nsplit = 4
    fb = ff // nsplit
    per = cap // nsplit
    o = None
    for n in range(nsplit):
        cols = slice(n * fb, (n + 1) * fb)
        hid = _silu(_dot(xb, wg_ref[0, 0, :, cols])) * _dot(xb, wu_ref[0, 0, :, cols])
        part = _dot(hid.astype(BF16), wd_ref[0, 0, cols, :])
        o = part if o is None else o + part
        for j0 in range(n * per, (n + 1) * per, group):
            gather_group(e_next, oth, j0)
            scatter_group(e_prev, oth, affp_ref, prev_scale, j0)
    for c in range(nchunk):
        outs[cur, pl.ds(c, cap, stride=nchunk), :] = o[:, c * LANES:(c + 1) * LANES]

    @pl.when(e == ne - 1)
    def _():
        def body(jb, carry):
            scatter_group(e, cur, affc_ref, 1.0, jb * group)
            return carry

        lax.fori_loop(0, cap // group, body, 0)
        writeback(r).start()

    @pl.when((e == ne - 1) & (r == pl.num_programs(0) - 1))
    def _():
        writeback(r).wait()


def _expert_ffn(u2, idx, aff, wg, wu, wd, layer):
    _, E, D, FF = wg.shape
    nchunk = D // LANES
    R, rows, _ = u2.shape
    NT = rows // nchunk
    cap = idx.shape[-1]
    return pl.pallas_call(
        functools.partial(_ffn_kernel, cap=cap, nchunk=nchunk),
        grid=(R, E),
        in_specs=[
            pl.BlockSpec((1, 1, E * cap), lambda r, e: (r, 0, 0), memory_space=pltpu.SMEM),
            pl.BlockSpec((1, 1, 1, NT), lambda r, e: (r, jnp.maximum(e - 1, 0), 0, 0), memory_space=pltpu.SMEM),
            pl.BlockSpec((1, 1, 1, NT), lambda r, e: (r, e, 0, 0), memory_space=pltpu.SMEM),
            pl.BlockSpec((1, rows, LANES), lambda r, e: (r, 0, 0), pipeline_mode=pl.Buffered(1)),
            pl.BlockSpec((1, 1, D, FF), lambda r, e: (layer, e, 0, 0)),
            pl.BlockSpec((1, 1, D, FF), lambda r, e: (layer, e, 0, 0)),
            pl.BlockSpec((1, 1, FF, D), lambda r, e: (layer, e, 0, 0)),
        ],
        out_specs=pl.BlockSpec(memory_space=pl.ANY),
        out_shape=jax.ShapeDtypeStruct((R, rows, LANES), F32),
        scratch_shapes=[
            pltpu.VMEM((2, cap * nchunk, LANES), F32),
            pltpu.VMEM((2, cap * nchunk, LANES), F32),
            pltpu.VMEM((rows, LANES), F32),
            pltpu.SemaphoreType.DMA(()),
        ],
        compiler_params=_cparams(("arbitrary", "arbitrary"), 60 * 1024 * 1024),
        name="expert_ffn",
    )(idx.reshape(R, 1, E * cap), aff.reshape(R, E, 1, NT), aff.reshape(R, E, 1, NT), u2, wg, wu, wd)


def _resid_kernel(x_ref, ff_ref, g2_ref, gn_ref, o_ref):
    n, d = x_ref.shape[1:]
    ff = _load_token_tiles(ff_ref.at[0], n, d // LANES)
    o_ref[0] = x_ref[0] + g2_ref[0] * (_rms(ff) * gn_ref[...])


def _ffn_residual(x1, ff, g2, gn, r0, nr):
    _, NT, D = x1.shape
    tm = TOK_TILE
    tok = lambda r, i: (r0 + r, i, 0)
    return pl.pallas_call(
        _resid_kernel,
        grid=(nr, NT // tm),
        in_specs=[
            pl.BlockSpec((1, tm, D), tok),
            pl.BlockSpec((1, tm * (D // LANES), LANES), tok),
            pl.BlockSpec((1, 1, D), lambda r, i: (r0 + r, 0, 0)),
            pl.BlockSpec((1, D), lambda r, i: (0, 0)),
        ],
        out_specs=pl.BlockSpec((1, tm, D), lambda r, i: (r, i, 0)),
        out_shape=jax.ShapeDtypeStruct((nr, NT, D), F32),
        compiler_params=_cparams(("parallel", "parallel")),
        name="ffn_residual",
    )(x1, ff, g2, gn)


def kernel(x_prompt, x_sample, cache_na_k, cache_na_v, state_hgrn, c, c_ctx, ada_w, ada_b,
           norm_pre_mix, norm_post_mix, norm_pre_ffn, norm_post_ffn, w_in, w_out, na_rpb,
           hg_lb, hg_norm, pool_w, pool_scale, w_router, w_gate, w_up, w_down):
    B, SEQ, D = x_prompt.shape
    NB, NT, _ = x_sample.shape
    L = ada_w.shape[0]
    assert B * SEQ == NT, "context tokens are stacked as one request of the latent sequence length"
    R = NB + 1
    na_w, hg_w = D // 2, D // 4
    H = na_w // HEAD_DIM
    nh = hg_w // HEAD_DIM
    E = w_router.shape[-1]
    past = cache_na_k.shape[2]
    cap_ctx = 2 * SEQ // E
    cap_lat = 2 * NT // E
    assert B * cap_ctx == cap_lat

    xa, xb = x_prompt.reshape(1, NT, D), x_sample
    RP = 16
    conds = jnp.zeros((RP, D), F32).at[0].set(c_ctx).at[1:R].set(c)
    mod = _modulation(conds, ada_w, ada_b)[:, :R].reshape(L, R, 6, 1, D)

    lb = jnp.cumsum(jax.nn.softmax(hg_lb.astype(F32), axis=0), axis=0)
    lb = lb - lb[:1]
    log_lb, log1m_lb = jnp.log(lb), jnp.log1p(-lb)

    ck = cache_na_k.reshape(NB, L, past, na_w)
    cv = cache_na_v.reshape(NB, L, past, na_w)
    seg = jnp.asarray(np.kron(np.eye(nh), np.full((HEAD_DIM, HEAD_DIM), 1.0 / HEAD_DIM)), BF16)
    eye_h = jnp.eye(nh, dtype=F32)
    s0_t = jnp.swapaxes(state_hgrn.astype(F32), -1, -2)
    s0_all = (s0_t[:, :, :, :, :, None, :] * eye_h[:, None, :, None]).reshape(NB, L, 2, hg_w, hg_w)

    wg_bf, wu_bf, wd_bf = w_gate.astype(BF16), w_up.astype(BF16), w_down.astype(BF16)
    tab = _na_bias_table(na_rpb)
    new_k, new_v, new_s = [], [], []
    for l in range(L):
        sh1, sc1, g1, sh2, sc2, g2 = (mod[l, :, j] for j in range(6))
        w_bf = w_in[l].astype(BF16)
        g0 = 3 * na_w + hg_w
        w_gate_lo = (w_in[l][:, g0:g0 + 2 * hg_w] - w_bf[:, g0:g0 + 2 * hg_w].astype(F32)).astype(BF16)
        q, k, v, kvf, hp = _inproj(xa, xb, norm_pre_mix[l][None], sc1, sh1, w_bf, w_gate_lo,
                                   log_lb[l], log1m_lb[l])
        new_k.append(kvf[:NT, :na_w].reshape(B, SEQ, H, HEAD_DIM))
        new_v.append(kvf[:NT, na_w:].reshape(B, SEQ, H, HEAD_DIM))

        ona = (_ctx_attention(q, k, v, B, SEQ), _na_attention(q, k, v, ck, cv, l, tab))

        of_c, ob_c, st_c = _hgrn(hp, B, SEQ)
        of_l, ob_l, _ = _hgrn(hp, NB, NT, s0=s0_all[:, l])
        st_c = st_c.reshape(B, 2, nh, HEAD_DIM, nh, HEAD_DIM)
        new_s.append(jnp.stack([jnp.swapaxes(st_c[:, :, h, :, h, :], -1, -2) for h in range(nh)], axis=2))

        w_bd = jnp.einsum('gcd,gh->gchd', pool_w[l].astype(F32), jnp.eye(len(POOL_HALF), dtype=F32))
        w_bd = w_bd.reshape(hg_w, hg_w).astype(BF16)
        opool = (_pool(hp, B, SEQ, w_bd, pool_scale[l][None], False),
                 _pool(hp, NB, NT, w_bd, pool_scale[l][None], True))

        wr_hi, wr_lo = _split_bf16(w_router[l].T.astype(F32))
        x1, u2, logits = _outproj((xa, xb), ona, (of_c, of_l), (ob_c, ob_l), opool, hp, w_out[l].astype(BF16),
                                  hg_norm[l][None], seg, g1, norm_post_mix[l][None], norm_pre_ffn[l][None],
                                  sc2, sh2, wr_hi, wr_lo)

        idx_c, aff_c = _router(logits[:1], B, SEQ, cap_ctx, False)
        idx_l, aff_l = _router_seg(logits[1:], cap_lat)
        idx_c = idx_c[0, :, :E * B].reshape(cap_ctx, E, B).transpose(1, 2, 0)
        idx_c = (idx_c + (jnp.arange(B, dtype=I32) * SEQ)[None, :, None]).reshape(1, E, cap_lat)
        idx = jnp.concatenate([idx_c, idx_l[:, :, :E].transpose(0, 2, 1)], axis=0)
        aff = jnp.concatenate([aff_c.reshape(1, E, NT), aff_l], axis=0)

        ff = _expert_ffn(u2, idx, aff, wg_bf, wu_bf, wd_bf, l)
        xa = _ffn_residual(x1, ff, g2, norm_post_ffn[l][None], 0, 1)
        xb = _ffn_residual(x1, ff, g2, norm_post_ffn[l][None], 1, NB)

    y_prompt = xa.reshape(B, SEQ, D)
    y_sample = xb
    new_cache_k = jnp.stack(new_k, axis=1)
    new_cache_v = jnp.stack(new_v, axis=1)
    new_state = jnp.stack(new_s, axis=1).astype(x_prompt.dtype)
    return (y_prompt, y_sample, new_cache_k, new_cache_v, new_state)
```

```python
import functools

import numpy as np
import jax
import jax.numpy as jnp
from jax import lax
from jax.experimental import pallas as pl
from jax.experimental.pallas import tpu as pltpu

F32 = jnp.float32
BF16 = jnp.bfloat16
I32 = jnp.int32

HEAD_DIM = 64
LOG_HEAD_DIM = 6
LANES = 128
GRID_W = 64
NA_KH = 8
NA_KW = 16
NA_RB = 8
NA_WR = 16
HG_CHUNK = 64
HG_SUB = 16
HG_CLAMP = 80.0
POOL_HALF = (1, 2, 4, 8)
N_EXPERTS = 16
EPS = 1e-6
NEG_BIG = -1e30
TOK_TILE = 512
VMEM_LIMIT = 56 * 1024 * 1024


def _cparams(sem, vmem=None):
    return pltpu.CompilerParams(dimension_semantics=sem, vmem_limit_bytes=vmem)


def _silu(x):
    return x / (1.0 + jnp.exp(-x))


def _dot(a, b):
    return jnp.dot(a, b, preferred_element_type=F32)


def _dot_nt(a, b):
    return lax.dot_general(a, b, (((1,), (1,)), ((), ())), preferred_element_type=F32)


def _dot_tn(a, b):
    return lax.dot_general(a, b, (((0,), (0,)), ((), ())), preferred_element_type=F32)


def _rms(x):
    return x * lax.rsqrt(jnp.mean(x * x, axis=-1, keepdims=True) + EPS)


def _store_token_tiles(ref, x):
    n, d = x.shape
    nchunk = d // LANES
    for c in range(nchunk):
        ref[0, pl.ds(c, n, stride=nchunk), :] = x[:, c * LANES:(c + 1) * LANES]


def _load_token_tiles(ref2d, n, nchunk):
    return jnp.concatenate([ref2d[pl.ds(c, n, stride=nchunk), :] for c in range(nchunk)], axis=1)


def _split_bf16(x):
    hi = x.astype(BF16)
    lo = (x - hi.astype(F32)).astype(BF16)
    return hi, lo


def _mod_kernel(c_ref, w_ref, b_ref, o_ref):
    ah, al = _split_bf16(_silu(c_ref[...]))
    wh, wl = _split_bf16(w_ref[0])
    o_ref[0] = _dot(ah, wh) + _dot(al, wh) + _dot(ah, wl) + b_ref[0]


def _modulation(conds, ada_w, ada_b):
    L, D, D6 = ada_w.shape
    RP = conds.shape[0]
    return pl.pallas_call(
        _mod_kernel,
        grid=(L, D6 // D),
        in_specs=[
            pl.BlockSpec((RP, D), lambda l, j: (0, 0)),
            pl.BlockSpec((1, D, D), lambda l, j: (l, 0, j)),
            pl.BlockSpec((1, 1, D), lambda l, j: (l, 0, j)),
        ],
        out_specs=pl.BlockSpec((1, RP, D), lambda l, j: (l, 0, j)),
        out_shape=jax.ShapeDtypeStruct((L, RP, D6), F32),
        compiler_params=_cparams(("parallel", "parallel")),
        name="modulation",
    )(conds, ada_w, ada_b.reshape(L, 1, D6))


def _log_gate(x, la, l1):
    y = l1 + jnp.minimum(x, 0.0) - jnp.log1p(jnp.exp(-jnp.abs(x)))
    return jnp.maximum(la, y) + jnp.log1p(jnp.exp(-jnp.abs(la - y)))


def _stacked_x(xa_ref, xb_ref):
    return jnp.where(pl.program_id(0) == 0, xa_ref[0], xb_ref[0])


def _stacked_x_specs(tm, D, nb):
    return [
        pl.BlockSpec((1, tm, D), lambda r, i: (0, jnp.where(r == 0, i, nb - 1), 0)),
        pl.BlockSpec((1, tm, D), lambda r, i: (jnp.maximum(r - 1, 0), jnp.where(r == 0, 0, i), 0)),
    ]


def _inproj_kernel(xa_ref, xb_ref, g_ref, sc_ref, sh_ref, w_ref, wlo_ref, la_ref, l1_ref,
                   q_ref, k_ref, v_ref, kvf_ref, hp_ref, *, na_w, hg_w):
    u = _rms(_stacked_x(xa_ref, xb_ref)) * g_ref[...] * (1.0 + sc_ref[0]) + sh_ref[0]
    uh, ul = _split_bf16(u)
    p = _dot(uh, w_ref[...])
    g0 = 3 * na_w + hg_w
    gates = p[:, g0:g0 + 2 * hg_w] + _dot(ul, w_ref[:, g0:g0 + 2 * hg_w]) + _dot(uh, wlo_ref[...])
    k = p[:, na_w:2 * na_w]
    v = p[:, 2 * na_w:3 * na_w]
    q_ref[0] = (p[:, :na_w] * (HEAD_DIM ** -0.5)).astype(BF16)
    k_ref[0] = k.astype(BF16)
    v_ref[0] = v.astype(BF16)
    kvf_ref[:, :na_w] = k
    kvf_ref[:, na_w:] = v
    o = 3 * na_w

    def col(j):
        return p[:, o + j * hg_w:o + (j + 1) * hg_w]

    hp_ref[0, :, 0 * hg_w:1 * hg_w] = _silu(col(0))
    hp_ref[0, :, 1 * hg_w:2 * hg_w] = col(3)
    hp_ref[0, :, 2 * hg_w:3 * hg_w] = _log_gate(gates[:, :hg_w], la_ref[0:1, :], l1_ref[0:1, :])
    hp_ref[0, :, 3 * hg_w:4 * hg_w] = _log_gate(gates[:, hg_w:], la_ref[1:2, :], l1_ref[1:2, :])
    hp_ref[0, :, 4 * hg_w:5 * hg_w] = _silu(col(4))
    hp_ref[0, :, 5 * hg_w:6 * hg_w] = col(5)


def _inproj(xa, xb, g, sc, sh, w_in_bf, w_gate_lo, log_lb, log1m_lb):
    _, NT, D = xa.shape
    R = xb.shape[0] + 1
    na_w, hg_w = D // 2, D // 4
    inw = w_in_bf.shape[1]
    tm = TOK_TILE
    nb = NT // tm
    kern = functools.partial(_inproj_kernel, na_w=na_w, hg_w=hg_w)
    tok = lambda r, i: (r, i, 0)
    per_req = lambda r, i: (r, 0, 0)
    const2 = lambda r, i: (0, 0)
    return pl.pallas_call(
        kern,
        grid=(R, nb),
        in_specs=_stacked_x_specs(tm, D, nb) + [
            pl.BlockSpec((1, D), const2),
            pl.BlockSpec((1, 1, D), per_req),
            pl.BlockSpec((1, 1, D), per_req),
            pl.BlockSpec((D, inw), const2),
            pl.BlockSpec((D, 2 * hg_w), const2),
            pl.BlockSpec((2, hg_w), const2),
            pl.BlockSpec((2, hg_w), const2),
        ],
        out_specs=[
            pl.BlockSpec((1, tm, na_w), tok),
            pl.BlockSpec((1, tm, na_w), tok),
            pl.BlockSpec((1, tm, na_w), tok),
            pl.BlockSpec((tm, 2 * na_w), lambda r, i: (jnp.where(r == 0, i, nb), 0)),
            pl.BlockSpec((1, tm, 6 * hg_w), tok),
        ],
        out_shape=[
            jax.ShapeDtypeStruct((R, NT, na_w), BF16),
            jax.ShapeDtypeStruct((R, NT, na_w), BF16),
            jax.ShapeDtypeStruct((R, NT, na_w), BF16),
            jax.ShapeDtypeStruct(((nb + 1) * tm, 2 * na_w), F32),
            jax.ShapeDtypeStruct((R, NT, 6 * hg_w), F32),
        ],
        compiler_params=_cparams(("arbitrary", "arbitrary"), VMEM_LIMIT),
        name="prenorm_inproj",
    )(xa, xb, g, sc, sh, w_in_bf, w_gate_lo, log_lb, log1m_lb)


def _pair_stack(q):
    first = lax.broadcasted_iota(I32, (1, LANES), 1) < HEAD_DIM
    zero = jnp.zeros_like(q)
    return jnp.concatenate([jnp.where(first, q, zero), jnp.where(first, zero, q)], axis=0), first


def _ctx_attn_kernel(q_ref, k_ref, v_ref, o_ref):
    n = q_ref.shape[1]
    q2, first = _pair_stack(q_ref[0])
    s = _dot_nt(q2, k_ref[0])
    p = jnp.exp(s - jnp.max(s, axis=-1, keepdims=True))
    l = jnp.sum(p, axis=-1, keepdims=True)
    o2 = _dot(p.astype(BF16), v_ref[0]) * (1.0 / l)
    o_ref[0] = jnp.where(first, o2[:n], o2[n:]).astype(BF16)


def _ctx_attention(q, k, v, nseq, seq):
    W = q.shape[-1]
    blk = pl.BlockSpec((1, seq, LANES), lambda s, p: (0, s, p))
    return pl.pallas_call(
        _ctx_attn_kernel,
        grid=(nseq, W // LANES),
        in_specs=[blk, blk, blk],
        out_specs=pl.BlockSpec((1, seq, LANES), lambda s, p: (0, s, p)),
        out_shape=jax.ShapeDtypeStruct((1, nseq * seq, W), BF16),
        compiler_params=_cparams(("parallel", "parallel")),
        name="ctx_attention",
    )(q, k, v)


def _na_bias_table(rpb):
    L, H = rpb.shape[:2]
    c = np.arange(GRID_W)[:, None]
    ck = np.arange(GRID_W)[None, :]
    wsc = np.clip(c - NA_KW // 2, 0, GRID_W - NA_KW)
    cvalid = (ck >= wsc) & (ck < wsc + NA_KW)
    nd, nx = 2 * NA_KH - 1, 2 * NA_KW - 1
    xsel = np.where(cvalid, np.clip(ck - c + NA_KW - 1, 0, nx - 1), nx)
    col_onehot = jnp.asarray(np.eye(nx + 1, dtype=np.float32)[xsel])
    row_onehot = np.zeros((NA_KH, NA_KH, nd), np.float32)
    for d in range(NA_KH):
        row_onehot[d, np.arange(NA_KH), d + np.arange(NA_KH)] = 1.0
    rpb_ext = jnp.concatenate([rpb.astype(F32), jnp.full((L, H, nd, 1), NEG_BIG, F32)], axis=3)
    ts = jnp.einsum('lhax,dja,ckx->lhdcjk', rpb_ext, jnp.asarray(row_onehot), col_onehot,
                    precision=lax.Precision.HIGHEST)
    return ts.reshape(L, H, NA_KH, GRID_W, NA_KH * GRID_W)


def _na_attn_kernel(q_ref, k_ref, v_ref, kc_ref, vc_ref, ts_ref, o_ref, pc_sc, *, rows):
    i = pl.program_id(2)
    nq = NA_RB * GRID_W
    nkl = NA_KH * GRID_W
    q2, first = _pair_stack(q_ref[0])
    kc = kc_ref[0, 0].astype(BF16)
    vc = vc_ref[0, 0].astype(BF16)
    s_c = _dot_nt(q2, kc)
    o_loc, l_inv = [], []
    for qr in range(NA_RB):
        r = NA_RB * i + qr
        rs = jnp.clip(r - NA_KH // 2, 0, rows - NA_KH)
        d0 = rs - r + NA_KH - 1
        start = pl.multiple_of(rs * GRID_W, GRID_W)
        a, b = qr * GRID_W, nq + qr * GRID_W
        q2r = jnp.concatenate([q2[a:a + GRID_W], q2[b:b + GRID_W]], axis=0)
        bias = jnp.concatenate([ts_ref[0, 0, d0], ts_ref[0, 1, d0]], axis=0)
        s_l = _dot_nt(q2r, k_ref[0, pl.ds(start, nkl), :]) + bias
        s_cr = jnp.concatenate([s_c[a:a + GRID_W], s_c[b:b + GRID_W]], axis=0)
        m = jnp.maximum(jnp.max(s_l, axis=-1, keepdims=True), jnp.max(s_cr, axis=-1, keepdims=True))
        p_l = jnp.exp(s_l - m)
        p_c = jnp.exp(s_cr - m)
        l_inv.append(1.0 / (jnp.sum(p_l, axis=-1, keepdims=True) + jnp.sum(p_c, axis=-1, keepdims=True)))
        o_loc.append(_dot(p_l.astype(BF16), v_ref[0, pl.ds(start, nkl), :]))
        pc_sc[2 * a:2 * a + 2 * GRID_W, :] = p_c.astype(BF16)
    o_ctx = _dot(pc_sc[...], vc)
    for qr in range(NA_RB):
        a = qr * GRID_W
        o2 = (o_loc[qr] + o_ctx[2 * a:2 * a + 2 * GRID_W]) * l_inv[qr]
        o_ref[0, a:a + GRID_W, :] = jnp.where(first, o2[:GRID_W], o2[GRID_W:]).astype(BF16)


def _na_attention(q, k, v, cache_k, cache_v, layer, tab):
    R, NT, W = q.shape
    nreq = R - 1
    rows = NT // GRID_W
    assert rows % NA_RB == 0 and rows >= NA_KH
    nb = rows // NA_RB
    past = cache_k.shape[2]
    nq = NA_RB * GRID_W
    kern = functools.partial(_na_attn_kernel, rows=rows)
    slab = pl.BlockSpec((1, NT, LANES), lambda b, p, i: (b + 1, 0, p))
    ctx = pl.BlockSpec((1, 1, past, LANES), lambda b, p, i: (b, layer, 0, p))
    return pl.pallas_call(
        kern,
        grid=(nreq, W // LANES, nb),
        in_specs=[
            pl.BlockSpec((1, nq, LANES), lambda b, p, i: (b + 1, i, p)),
            slab, slab, ctx, ctx,
            pl.BlockSpec((1, 2, NA_KH, GRID_W, NA_KH * GRID_W), lambda b, p, i: (layer, p, 0, 0, 0)),
        ],
        out_specs=pl.BlockSpec((1, nq, LANES), lambda b, p, i: (b, i, p)),
        out_shape=jax.ShapeDtypeStruct((nreq, NT, W), BF16),
        scratch_shapes=[pltpu.VMEM((2 * nq, past), BF16)],
        compiler_params=_cparams(("parallel", "parallel", "arbitrary"), VMEM_LIMIT),
        name="na_attention",
    )(q, k, v, cache_k, cache_v, tab)


def _hgrn_chunk(q, v, lf, st, rev):
    C, HW = q.shape
    nh = HW // HEAD_DIM
    nsub = C // HG_SUB
    row = lax.broadcasted_iota(I32, (C, 1), 0)
    tau = (C - 1 - row) if rev else row
    b = lf
    d = 1
    while d < C:
        shifted = pltpu.roll(b, (C - d) if rev else d, axis=0)
        b = b + jnp.where(tau >= d, shifted, 0.0)
        d *= 2
    kk = 1.0 - jnp.exp(lf)
    last = 0 if rev else C - 1
    b_last = b[last:last + 1, :]
    q_inter = q * jnp.exp(b)
    k_end = kk * jnp.exp(b_last - b)

    lane_head = lax.broadcasted_iota(I32, (1, HW), 1) >> LOG_HEAD_DIM
    vb = v.astype(BF16)
    a_rows = []
    starts = []
    for i in range(nsub):
        mid_tau = HG_SUB * i + HG_SUB // 2 - 1
        mid = (C - 1 - mid_tau) if rev else mid_tau
        b_mid = b[mid:mid + 1, :]
        ps = (C - HG_SUB * (i + 1)) if rev else HG_SUB * i
        starts.append(ps)
        qs = q[ps:ps + HG_SUB, :] * jnp.exp(jnp.minimum(b[ps:ps + HG_SUB, :] - b_mid, HG_CLAMP))
        ks = (kk * jnp.exp(jnp.minimum(b_mid - b, HG_CLAMP))).astype(BF16)
        lhs = jnp.concatenate([jnp.where(lane_head == h, qs, 0.0) for h in range(nh)], axis=0).astype(BF16)
        a = _dot_nt(lhs, ks)
        t_sub = lax.broadcasted_iota(I32, (nh * HG_SUB, C), 0) & (HG_SUB - 1)
        t_tau = HG_SUB * i + ((HG_SUB - 1 - t_sub) if rev else t_sub)
        s_idx = lax.broadcasted_iota(I32, (nh * HG_SUB, C), 1)
        s_tau = (C - 1 - s_idx) if rev else s_idx
        a_rows.append(jnp.where(s_tau <= t_tau, a, 0.0).astype(BF16))
    av = _dot(jnp.concatenate(a_rows, axis=0), vb)
    pieces = [None] * nsub
    for i in range(nsub):
        acc = jnp.zeros((HG_SUB, HW), F32)
        for h in range(nh):
            r0 = (i * nh + h) * HG_SUB
            acc = acc + jnp.where(lane_head == h, av[r0:r0 + HG_SUB, :], 0.0)
        pieces[starts[i] // HG_SUB] = acc
    o = jnp.concatenate(pieces, axis=0) + _dot_nt(q_inter.astype(BF16), st.astype(BF16))

    rh = lax.broadcasted_iota(I32, (HW, HW), 0) >> LOG_HEAD_DIM
    ch = lax.broadcasted_iota(I32, (HW, HW), 1) >> LOG_HEAD_DIM
    upd = _dot_tn(vb, k_end.astype(BF16))
    st_new = st * jnp.exp(b_last) + jnp.where(rh == ch, upd, 0.0)
    return o, st_new


def _hgrn_kernel(*refs, has_init, ns):
    seqs = [refs[6 * k:6 * k + 6] for k in range(ns)]
    rest = refs[6 * ns:]
    if has_init:
        s0, of, ob, sout, st = rest
    else:
        of, ob, sout, st = rest
        s0 = None
    c = pl.program_id(1)

    @pl.when(c == 0)
    def _():
        if has_init:
            st[...] = s0[...]
        else:
            st[...] = jnp.zeros_like(st)

    for k, (qf, vf, lff, qb, vb_, lfb) in enumerate(seqs):
        o, s_new = _hgrn_chunk(qf[0], vf[0], lff[0], st[k, 0], rev=False)
        of[k] = o
        st[k, 0] = s_new
        o, s_new = _hgrn_chunk(qb[0], vb_[0], lfb[0], st[k, 1], rev=True)
        ob[k] = o
        st[k, 1] = s_new

    @pl.when(c == pl.num_programs(1) - 1)
    def _():
        sout[...] = st[...]


def _hgrn(hp, nseq, n, s0=None):
    R, NT, W6 = hp.shape
    HW = W6 // 6
    nc = n // HG_CHUNK
    latent = s0 is not None
    ns = next(k for k in (4, 2, 1) if nseq % k == 0) if latent else 1

    def in_spec(colblk, back, k):
        def index(g, c):
            cc = nc - 1 - c if back else c
            return (1 + ns * g + k, cc, colblk) if latent else (0, g * nc + cc, colblk)
        return pl.BlockSpec((1, HG_CHUNK, HW), index)

    def out_spec(back):
        def index(g, c):
            cc = nc - 1 - c if back else c
            return (g, cc, 0) if latent else (0, g * nc + cc, 0)
        return pl.BlockSpec((ns, HG_CHUNK, HW), index)

    in_specs, args = [], []
    for k in range(ns):
        in_specs += [in_spec(0, False, k), in_spec(1, False, k), in_spec(2, False, k),
                     in_spec(0, True, k), in_spec(1, True, k), in_spec(3, True, k)]
        args += [hp] * 6
    state_spec = pl.BlockSpec((ns, 2, HW, HW), lambda g, c: (g, 0, 0, 0))
    if latent:
        in_specs.append(state_spec)
        args.append(s0)
    nout = nseq if latent else 1
    return pl.pallas_call(
        functools.partial(_hgrn_kernel, has_init=latent, ns=ns),
        grid=(nseq // ns, nc),
        in_specs=in_specs,
        out_specs=[out_spec(False), out_spec(True), state_spec],
        out_shape=[
            jax.ShapeDtypeStruct((nout, NT, HW), F32),
            jax.ShapeDtypeStruct((nout, NT, HW), F32),
            jax.ShapeDtypeStruct((nseq, 2, HW, HW), F32),
        ],
        scratch_shapes=[pltpu.VMEM((ns, 2, HW, HW), F32)],
        compiler_params=_cparams(("parallel", "arbitrary")),
        name="hgrn_scan",
    )(*args)


def _pool_kernel(a_ref, w_ref, sc_ref, o_ref):
    a = a_ref[0]
    n, W = a.shape
    t = lax.broadcasted_iota(I32, (n, 1), 0)

    def shift(x, d):
        src = t - d
        return jnp.where((src >= 0) & (src < n), pltpu.roll(x, d % n, axis=0), 0.0)

    trail = [a]
    fwd = [a]
    for j in range(len(POOL_HALF) - 1):
        h = POOL_HALF[j]
        trail.append(trail[j] + shift(trail[j], h))
        fwd.append(fwd[j] + shift(fwd[j], -h))
    grp = lax.broadcasted_iota(I32, (1, W), 1) >> LOG_HEAD_DIM
    win = jnp.zeros_like(a)
    half = jnp.zeros((1, W), I32)
    for j, h in enumerate(POOL_HALF):
        win = jnp.where(grp == j, shift(trail[j], 1) + fwd[j], win)
        half = jnp.where(grp == j, h, half)
    cnt = jnp.minimum(t + half, n) - jnp.maximum(t - half, 0)
    p = win / cnt.astype(F32) - a
    o_ref[0] = (_dot(p.astype(BF16), w_ref[...]) * sc_ref[...]).astype(BF16)


def _pool(hp, nseq, n, w_bd, scale, latent):
    R, NT, W6 = hp.shape
    W = W6 // 6
    return pl.pallas_call(
        _pool_kernel,
        grid=(nseq,),
        in_specs=[
            pl.BlockSpec((1, n, W), (lambda s: (s + 1, 0, 5)) if latent else (lambda s: (0, s, 5))),
            pl.BlockSpec((W, W), lambda s: (0, 0)),
            pl.BlockSpec((1, W), lambda s: (0, 0)),
        ],
        out_specs=pl.BlockSpec((1, n, W), (lambda s: (s, 0, 0)) if latent else (lambda s: (0, s, 0))),
        out_shape=jax.ShapeDtypeStruct((nseq if latent else 1, NT, W), BF16),
        compiler_params=_cparams(("parallel",), VMEM_LIMIT),
        name="pool_mixer",
    )(hp, w_bd, scale)


def _outproj_kernel(xa_ref, xb_ref, na_a, na_b, of_a, of_b, ob_a, ob_b, op_a, op_b, sg_ref,
                    wo_ref, hgn_ref, seg_ref, g1_ref, gpm_ref, gpf_ref, sc2_ref, sh2_ref, wrh_ref, wrl_ref,
                    x1_ref, u2_ref, lg_ref, *, na_w, hg_w):
    tm, d = x1_ref.shape[1:]
    nchunk = d // LANES
    nsub = 2
    sub = tm // nsub
    first = pl.program_id(0) == 0
    for h in range(nsub):
        rows = slice(h * sub, (h + 1) * sub)

        def pick(a_ref, b_ref):
            return jnp.where(first, a_ref[0, rows, :], b_ref[0, rows, :])

        o = pick(of_a, of_b) + pick(ob_a, ob_b)
        hi, lo = _split_bf16(o * o)
        ms = _dot(hi, seg_ref[...]) + _dot(lo, seg_ref[...])
        ohg = o * lax.rsqrt(ms + EPS) * hgn_ref[...] * sg_ref[0, rows, :]
        mix = (_dot(pick(na_a, na_b), wo_ref[0:na_w, :])
               + _dot(ohg.astype(BF16), wo_ref[na_w:na_w + hg_w, :])
               + _dot(pick(op_a, op_b), wo_ref[na_w + hg_w:, :]))
        x1 = pick(xa_ref, xb_ref) + g1_ref[0] * (_rms(mix) * gpm_ref[...])
        x1_ref[0, rows, :] = x1
        u2 = _rms(x1) * gpf_ref[...] * (1.0 + sc2_ref[0]) + sh2_ref[0]
        for c in range(nchunk):
            u2_ref[0, pl.ds(h * sub * nchunk + c, sub, stride=nchunk), :] = u2[:, c * LANES:(c + 1) * LANES]
        uh, ul = _split_bf16(u2)
        lg_ref[0, :, rows] = _dot_nt(wrh_ref[...], uh) + _dot_nt(wrl_ref[...], uh) + _dot_nt(wrh_ref[...], ul)


def _outproj(x, ona, of, ob, opool, hp, wo_bf, hg_norm, seg, g1, gpm, gpf, sc2, sh2, wr_hi, wr_lo):
    _, NT, D = x[0].shape
    R = x[1].shape[0] + 1
    na_w, hg_w = D // 2, D // 4
    E = wr_hi.shape[0]
    tm = TOK_TILE
    nb = NT // tm
    kern = functools.partial(_outproj_kernel, na_w=na_w, hg_w=hg_w)
    tok = lambda r, i: (r, i, 0)
    per_req = lambda r, i: (r, 0, 0)
    const2 = lambda r, i: (0, 0)
    pairs = (_stacked_x_specs(tm, D, nb) + _stacked_x_specs(tm, na_w, nb) + _stacked_x_specs(tm, hg_w, nb)
             + _stacked_x_specs(tm, hg_w, nb) + _stacked_x_specs(tm, hg_w, nb))
    return pl.pallas_call(
        kern,
        grid=(R, nb),
        in_specs=pairs + [
            pl.BlockSpec((1, tm, hg_w), lambda r, i: (r, i, 4)),
            pl.BlockSpec((D, D), const2),
            pl.BlockSpec((1, hg_w), const2),
            pl.BlockSpec((hg_w, hg_w), const2),
            pl.BlockSpec((1, 1, D), per_req),
            pl.BlockSpec((1, D), const2),
            pl.BlockSpec((1, D), const2),
            pl.BlockSpec((1, 1, D), per_req),
            pl.BlockSpec((1, 1, D), per_req),
            pl.BlockSpec((E, D), const2),
            pl.BlockSpec((E, D), const2),
        ],
        out_specs=[
            pl.BlockSpec((1, tm, D), tok),
            pl.BlockSpec((1, tm * (D // LANES), LANES), tok),
            pl.BlockSpec((1, E, tm), lambda r, i: (r, 0, i)),
        ],
        out_shape=[
            jax.ShapeDtypeStruct((R, NT, D), F32),
            jax.ShapeDtypeStruct((R, NT * (D // LANES), LANES), F32),
            jax.ShapeDtypeStruct((R, E, NT), F32),
        ],
        compiler_params=_cparams(("parallel", "parallel"), VMEM_LIMIT),
        name="outproj_router",
    )(*x, *ona, *of, *ob, *opool, hp, wo_bf, hg_norm, seg, g1, gpm, gpf, sc2, sh2, wr_hi, wr_lo)


def _prefix_lanes(mask_f, tri):
    rows, n = mask_f.shape
    T = tri.shape[0]
    outs = []
    carry = jnp.zeros((rows, 1), F32)
    for j in range(n // T):
        seg = mask_f[:, j * T:(j + 1) * T]
        pre = _dot(seg.astype(BF16), tri) + carry
        outs.append(pre)
        carry = pre[:, T - 1:T]
    return outs[0] if len(outs) == 1 else jnp.concatenate(outs, axis=1)


def _router_kernel(lg_ref, tri_ref, idx_ref, aff_ref, rank_sc, *, cap, nrow_groups):
    if nrow_groups == 1:
        lg = lg_ref[0]
        ex = jnp.exp(lg - jnp.max(lg, axis=0, keepdims=True))
        aff = ex / jnp.sum(ex, axis=0, keepdims=True)
    else:
        lg = lg_ref[...]
        ex = jnp.exp(lg - jnp.max(lg, axis=0, keepdims=True))
        aff = ex / jnp.sum(ex, axis=0, keepdims=True)
        aff = aff.reshape(lg.shape[0] * lg.shape[1], lg.shape[2])
    rows, n = aff.shape
    aff_ref[...] = aff.reshape(aff_ref.shape)
    thr = jnp.zeros((rows, 1), I32)
    for bit in range(30, -1, -1):
        cand = thr | (1 << bit)
        cnt = jnp.sum((aff >= pltpu.bitcast(cand, F32)).astype(F32), axis=-1, keepdims=True)
        thr = jnp.where(cnt >= cap, cand, thr)
    thr_f = pltpu.bitcast(thr, F32)
    gt = aff > thr_f
    eq = aff == thr_f
    need = cap - jnp.sum(gt.astype(F32), axis=-1, keepdims=True)
    tri = tri_ref[...]
    eq_f = eq.astype(F32)
    eq_before = _prefix_lanes(eq_f, tri) - eq_f
    sel = gt | (eq & (eq_before < need))
    rank_sc[...] = _prefix_lanes(sel.astype(F32), tri)

    ncol = idx_ref.shape[-1]
    jcol = lax.broadcasted_iota(I32, (cap, 1), 0).astype(F32)
    lane = lax.broadcasted_iota(I32, (1, ncol), 1)

    def body(r, acc):
        rk = rank_sc[pl.ds(r, 1), :]
        pos = jnp.sum((rk <= jcol).astype(F32), axis=-1, keepdims=True)
        return jnp.where(lane == r, pos, acc)

    idx = lax.fori_loop(0, rows, body, jnp.zeros((cap, ncol), F32))
    idx_ref[...] = idx.astype(I32).reshape(idx_ref.shape)


def _router_seg_kernel(lg_ref, tri_ref, mexc_ref, gsel_ref, idx_ref, aff_ref, loc_sc, pinc_sc, pexc_sc, *, cap):
    lg = lg_ref[0]
    E, S, _ = lg.shape
    rows = E * S
    log_s = S.bit_length() - 1
    ex = jnp.exp(lg - jnp.max(lg, axis=0, keepdims=True))
    aff = ex / jnp.sum(ex, axis=0, keepdims=True)
    aff_ref[0] = aff
    thr = jnp.zeros((E, 1, 1), I32)
    for bit in range(30, -1, -1):
        cand = thr | (1 << bit)
        cnt = jnp.sum((aff >= pltpu.bitcast(cand, F32)).astype(F32), axis=(1, 2), keepdims=True)
        thr = jnp.where(cnt >= cap, cand, thr)
    thr_f = pltpu.bitcast(thr, F32)
    gt = aff > thr_f
    eq = aff == thr_f
    need = cap - jnp.sum(gt.astype(F32), axis=(1, 2), keepdims=True)
    tri = tri_ref[...]
    eq2 = eq.astype(F32).reshape(rows, LANES)
    eq_loc = _dot(eq2.astype(BF16), tri)
    eq_tot = jnp.broadcast_to(eq_loc[:, LANES - 1:LANES], (rows, LANES)).astype(BF16)
    eq_before = (eq_loc - eq2 + _dot(mexc_ref[...], eq_tot)).reshape(E, S, LANES)
    sel = (gt | (eq & (eq_before < need))).astype(BF16).reshape(rows, LANES)
    loc_sc[0:rows, :] = _dot(sel, tri)
    loc_sc[rows:, :] = jnp.zeros((LANES, LANES), F32)
    tot_l = _dot_nt(jnp.ones((E, LANES), BF16), sel)
    own = (lax.broadcasted_iota(I32, (E, rows), 1) >> log_s) == lax.broadcasted_iota(I32, (E, rows), 0)
    tot_e = _dot(jnp.where(own, tot_l, 0.0).astype(BF16), gsel_ref[...])
    pinc = _dot(tot_e.astype(BF16), tri)
    pinc_sc[...] = pinc
    pexc_sc[...] = pinc - tot_e
    jcol = lax.broadcasted_iota(I32, (cap, 1), 0).astype(F32)
    lane = lax.broadcasted_iota(I32, (1, idx_ref.shape[-1]), 1)

    def body(e, acc):
        pinc_e = pinc_sc[pl.ds(e, 1), :]
        pexc_e = pexc_sc[pl.ds(e, 1), :]
        inside = (pexc_e <= jcol) & (jcol < pinc_e)
        nfull = jnp.sum((pinc_e <= jcol).astype(F32), axis=-1, keepdims=True)
        before = jnp.sum(jnp.where(inside, pexc_e, 0.0), axis=-1, keepdims=True)
        loc_e = loc_sc[pl.ds(pl.multiple_of(e * S, 8), LANES), :].astype(BF16)
        rsel = _dot(inside.astype(BF16), loc_e)
        cnt = jnp.sum((rsel <= jcol - before).astype(F32), axis=-1, keepdims=True)
        return jnp.where(lane == e, LANES * nfull + cnt, acc)

    idx = lax.fori_loop(0, E, body, jnp.zeros((cap, idx_ref.shape[-1]), F32))
    idx_ref[0] = idx.astype(I32)


def _router_seg(logits, cap):
    G, E, N = logits.shape
    S = N // LANES
    assert S & (S - 1) == 0
    rows = E * S
    tri = jnp.asarray(np.triu(np.ones((LANES, LANES), np.float32)), BF16)
    mexc = jnp.asarray(np.kron(np.eye(E), np.tril(np.ones((S, S)), -1)), BF16)
    gsel = jnp.asarray(np.tile(np.eye(S, LANES), (E, 1)), BF16)
    idx, aff = pl.pallas_call(
        functools.partial(_router_seg_kernel, cap=cap),
        grid=(G,),
        in_specs=[
            pl.BlockSpec((1, E, S, LANES), lambda g: (g, 0, 0, 0)),
            pl.BlockSpec((LANES, LANES), lambda g: (0, 0)),
            pl.BlockSpec((rows, rows), lambda g: (0, 0)),
            pl.BlockSpec((rows, LANES), lambda g: (0, 0)),
        ],
        out_specs=[pl.BlockSpec((1, cap, LANES), lambda g: (g, 0, 0)),
                   pl.BlockSpec((1, E, S, LANES), lambda g: (g, 0, 0, 0))],
        out_shape=[jax.ShapeDtypeStruct((G, cap, LANES), I32), jax.ShapeDtypeStruct((G, E, S, LANES), F32)],
        scratch_shapes=[pltpu.VMEM((rows + LANES, LANES), F32), pltpu.VMEM((E, LANES), F32),
                        pltpu.VMEM((E, LANES), F32)],
        compiler_params=_cparams(("parallel",), VMEM_LIMIT),
        name="router_topk_seg",
    )(logits.reshape(G, E, S, LANES), tri, mexc, gsel)
    return idx, aff.reshape(G, E, N)


def _router(logits, nq, n, cap, per_step_requests):
    G, E, NTOT = logits.shape
    rows = E * nq
    ncol = max(LANES, rows)
    T = min(n, 2 * LANES)
    tri = jnp.asarray(np.triu(np.ones((T, T), np.float32)), BF16)
    kern = functools.partial(_router_kernel, cap=cap, nrow_groups=nq)
    if nq == 1:
        lg_spec = pl.BlockSpec((1, E, n), lambda g: (g, 0, 0))
        lg_in = logits
        aff_spec = pl.BlockSpec((1, E, n), lambda g: (g, 0, 0))
        aff_shape = jax.ShapeDtypeStruct((G, E, n), F32)
    else:
        assert G == 1
        lg_in = logits.reshape(E, nq, n)
        lg_spec = pl.BlockSpec((E, nq, n), lambda g: (0, 0, 0))
        aff_spec = pl.BlockSpec((rows, n), lambda g: (0, 0))
        aff_shape = jax.ShapeDtypeStruct((rows, n), F32)
    return pl.pallas_call(
        kern,
        grid=(G,),
        in_specs=[lg_spec, pl.BlockSpec((T, T), lambda g: (0, 0))],
        out_specs=[pl.BlockSpec((1, cap, ncol), lambda g: (g, 0, 0)), aff_spec],
        out_shape=[jax.ShapeDtypeStruct((G, cap, ncol), I32), aff_shape],
        scratch_shapes=[pltpu.VMEM((rows, n), F32)],
        compiler_params=_cparams(("parallel",), VMEM_LIMIT),
        name="router_topk",
    )(lg_in, tri)


def _ffn_kernel(idx_ref, affp_ref, affc_ref, u_ref, wg_ref, wu_ref, wd_ref, y_hbm, xs, outs, y_acc, sem,
                *, cap, nchunk):
    r = pl.program_id(0)
    e = pl.program_id(1)
    ne = pl.num_programs(1)
    group = 2
    loop_group = 8
    cur = e & 1
    oth = 1 - cur

    def tile(i):
        if isinstance(i, int):
            return pl.ds(i * nchunk, nchunk)
        return pl.ds(pl.multiple_of(i * nchunk, nchunk), nchunk)

    def gather_group(expert, buf, j0, n=group):
        for u in range(n):
            xs[buf, tile(j0 + u), :] = u_ref[0, tile(idx_ref[0, 0, expert * cap + j0 + u]), :]

    def scatter_group(expert, buf, gate_ref, scale, j0, n=group):
        ts = [idx_ref[0, 0, expert * cap + j0 + u] for u in range(n)]
        new = [y_acc[tile(t), :] + outs[buf, tile(j0 + u), :] * (gate_ref[0, 0, 0, t] * scale)
               for u, t in enumerate(ts)]
        for t, val in zip(ts, new):
            y_acc[tile(t), :] = val

    def writeback(req):
        return pltpu.make_async_copy(y_acc, y_hbm.at[req], sem)

    @pl.when((e == 0) & (r > 0))
    def _():
        writeback(r - 1).wait()

    @pl.when(e == 0)
    def _():
        y_acc[...] = jnp.zeros_like(y_acc)
        outs[...] = jnp.zeros_like(outs)

        def body(jb, carry):
            gather_group(0, 0, jb * loop_group, loop_group)
            return carry

        lax.fori_loop(0, cap // loop_group, body, 0)

    e_next = jnp.minimum(e + 1, ne - 1)
    e_prev = jnp.maximum(e - 1, 0)
    prev_scale = jnp.where(e > 0, 1.0, 0.0)
    xb = _load_token_tiles(xs.at[cur], cap, nchunk).astype(BF16)
    ff = wg_ref.shape[-1]
    nsplit = 4
    fb = ff // nsplit
    per = cap // nsplit
    o = None
    for n in range(nsplit):
        cols = slice(n * fb, (n + 1) * fb)
        hid = _silu(_dot(xb, wg_ref[0, 0, :, cols])) * _dot(xb, wu_ref[0, 0, :, cols])
        part = _dot(hid.astype(BF16), wd_ref[0, 0, cols, :])
        o = part if o is None else o + part
        for j0 in range(n * per, (n + 1) * per, group):
            gather_group(e_next, oth, j0)
            scatter_group(e_prev, oth, affp_ref, prev_scale, j0)
    for c in range(nchunk):
        outs[cur, pl.ds(c, cap, stride=nchunk), :] = o[:, c * LANES:(c + 1) * LANES]

    @pl.when(e == ne - 1)
    def _():
        def body(jb, carry):
            scatter_group(e, cur, affc_ref, 1.0, jb * loop_group, loop_group)
            return carry

        lax.fori_loop(0, cap // loop_group, body, 0)
        writeback(r).start()

    @pl.when((e == ne - 1) & (r == pl.num_programs(0) - 1))
    def _():
        writeback(r).wait()


def _expert_ffn(u2, idx, aff, wg, wu, wd, layer):
    _, E, D, FF = wg.shape
    nchunk = D // LANES
    R, rows, _ = u2.shape
    NT = rows // nchunk
    cap = idx.shape[-1]
    return pl.pallas_call(
        functools.partial(_ffn_kernel, cap=cap, nchunk=nchunk),
        grid=(R, E),
        in_specs=[
            pl.BlockSpec((1, 1, E * cap), lambda r, e: (r, 0, 0), memory_space=pltpu.SMEM),
            pl.BlockSpec((1, 1, 1, NT), lambda r, e: (r, jnp.maximum(e - 1, 0), 0, 0), memory_space=pltpu.SMEM),
            pl.BlockSpec((1, 1, 1, NT), lambda r, e: (r, e, 0, 0), memory_space=pltpu.SMEM),
            pl.BlockSpec((1, rows, LANES), lambda r, e: (r, 0, 0), pipeline_mode=pl.Buffered(1)),
            pl.BlockSpec((1, 1, D, FF), lambda r, e: (layer, e, 0, 0)),
            pl.BlockSpec((1, 1, D, FF), lambda r, e: (layer, e, 0, 0)),
            pl.BlockSpec((1, 1, FF, D), lambda r, e: (layer, e, 0, 0)),
        ],
        out_specs=pl.BlockSpec(memory_space=pl.ANY),
        out_shape=jax.ShapeDtypeStruct((R, rows, LANES), F32),
        scratch_shapes=[
            pltpu.VMEM((2, cap * nchunk, LANES), F32),
            pltpu.VMEM((2, cap * nchunk, LANES), F32),
            pltpu.VMEM((rows, LANES), F32),
            pltpu.SemaphoreType.DMA(()),
        ],
        compiler_params=_cparams(("arbitrary", "arbitrary"), 60 * 1024 * 1024),
        name="expert_ffn",
    )(idx.reshape(R, 1, E * cap), aff.reshape(R, E, 1, NT), aff.reshape(R, E, 1, NT), u2, wg, wu, wd)


def _resid_kernel(x_ref, ff_ref, g2_ref, gn_ref, o_ref):
    n, d = x_ref.shape[1:]
    ff = _load_token_tiles(ff_ref.at[0], n, d // LANES)
    o_ref[0] = x_ref[0] + g2_ref[0] * (_rms(ff) * gn_ref[...])


def _ffn_residual(x1, ff, g2, gn, r0, nr):
    _, NT, D = x1.shape
    tm = TOK_TILE
    tok = lambda r, i: (r0 + r, i, 0)
    return pl.pallas_call(
        _resid_kernel,
        grid=(nr, NT // tm),
        in_specs=[
            pl.BlockSpec((1, tm, D), tok),
            pl.BlockSpec((1, tm * (D // LANES), LANES), tok),
            pl.BlockSpec((1, 1, D), lambda r, i: (r0 + r, 0, 0)),
            pl.BlockSpec((1, D), lambda r, i: (0, 0)),
        ],
        out_specs=pl.BlockSpec((1, tm, D), lambda r, i: (r, i, 0)),
        out_shape=jax.ShapeDtypeStruct((nr, NT, D), F32),
        compiler_params=_cparams(("parallel", "parallel")),
        name="ffn_residual",
    )(x1, ff, g2, gn)


def kernel(x_prompt, x_sample, cache_na_k, cache_na_v, state_hgrn, c, c_ctx, ada_w, ada_b,
           norm_pre_mix, norm_post_mix, norm_pre_ffn, norm_post_ffn, w_in, w_out, na_rpb,
           hg_lb, hg_norm, pool_w, pool_scale, w_router, w_gate, w_up, w_down):
    B, SEQ, D = x_prompt.shape
    NB, NT, _ = x_sample.shape
    L = ada_w.shape[0]
    assert B * SEQ == NT, "context tokens are stacked as one request of the latent sequence length"
    R = NB + 1
    na_w, hg_w = D // 2, D // 4
    H = na_w // HEAD_DIM
    nh = hg_w // HEAD_DIM
    E = w_router.shape[-1]
    past = cache_na_k.shape[2]
    cap_ctx = 2 * SEQ // E
    cap_lat = 2 * NT // E
    assert B * cap_ctx == cap_lat

    xa, xb = x_prompt.reshape(1, NT, D), x_sample
    RP = 16
    conds = jnp.zeros((RP, D), F32).at[0].set(c_ctx).at[1:R].set(c)
    mod = _modulation(conds, ada_w, ada_b)[:, :R].reshape(L, R, 6, 1, D)

    lb = jnp.cumsum(jax.nn.softmax(hg_lb.astype(F32), axis=0), axis=0)
    lb = lb - lb[:1]
    log_lb, log1m_lb = jnp.log(lb), jnp.log1p(-lb)

    ck = cache_na_k.reshape(NB, L, past, na_w)
    cv = cache_na_v.reshape(NB, L, past, na_w)
    seg = jnp.asarray(np.kron(np.eye(nh), np.full((HEAD_DIM, HEAD_DIM), 1.0 / HEAD_DIM)), BF16)
    eye_h = jnp.eye(nh, dtype=F32)
    s0_t = jnp.swapaxes(state_hgrn.astype(F32), -1, -2)
    s0_all = (s0_t[:, :, :, :, :, None, :] * eye_h[:, None, :, None]).reshape(NB, L, 2, hg_w, hg_w)

    wg_bf, wu_bf, wd_bf = w_gate.astype(BF16), w_up.astype(BF16), w_down.astype(BF16)
    tab = _na_bias_table(na_rpb)
    new_k, new_v, new_s = [], [], []
    for l in range(L):
        sh1, sc1, g1, sh2, sc2, g2 = (mod[l, :, j] for j in range(6))
        w_bf = w_in[l].astype(BF16)
        g0 = 3 * na_w + hg_w
        w_gate_lo = (w_in[l][:, g0:g0 + 2 * hg_w] - w_bf[:, g0:g0 + 2 * hg_w].astype(F32)).astype(BF16)
        q, k, v, kvf, hp = _inproj(xa, xb, norm_pre_mix[l][None], sc1, sh1, w_bf, w_gate_lo,
                                   log_lb[l], log1m_lb[l])
        new_k.append(kvf[:NT, :na_w].reshape(B, SEQ, H, HEAD_DIM))
        new_v.append(kvf[:NT, na_w:].reshape(B, SEQ, H, HEAD_DIM))

        ona = (_ctx_attention(q, k, v, B, SEQ), _na_attention(q, k, v, ck, cv, l, tab))

        of_c, ob_c, st_c = _hgrn(hp, B, SEQ)
        of_l, ob_l, _ = _hgrn(hp, NB, NT, s0=s0_all[:, l])
        st_c = st_c.reshape(B, 2, nh, HEAD_DIM, nh, HEAD_DIM)
        new_s.append(jnp.stack([jnp.swapaxes(st_c[:, :, h, :, h, :], -1, -2) for h in range(nh)], axis=2))

        w_bd = jnp.einsum('gcd,gh->gchd', pool_w[l].astype(F32), jnp.eye(len(POOL_HALF), dtype=F32))
        w_bd = w_bd.reshape(hg_w, hg_w).astype(BF16)
        opool = (_pool(hp, B, SEQ, w_bd, pool_scale[l][None], False),
                 _pool(hp, NB, NT, w_bd, pool_scale[l][None], True))

        wr_hi, wr_lo = _split_bf16(w_router[l].T.astype(F32))
        x1, u2, logits = _outproj((xa, xb), ona, (of_c, of_l), (ob_c, ob_l), opool, hp, w_out[l].astype(BF16),
                                  hg_norm[l][None], seg, g1, norm_post_mix[l][None], norm_pre_ffn[l][None],
                                  sc2, sh2, wr_hi, wr_lo)

        idx_c, aff_c = _router(logits[:1], B, SEQ, cap_ctx, False)
        idx_l, aff_l = _router_seg(logits[1:], cap_lat)
        idx_c = idx_c[0, :, :E * B].reshape(cap_ctx, E, B).transpose(1, 2, 0)
        idx_c = (idx_c + (jnp.arange(B, dtype=I32) * SEQ)[None, :, None]).reshape(1, E, cap_lat)
        idx = jnp.concatenate([idx_c, idx_l[:, :, :E].transpose(0, 2, 1)], axis=0)
        aff = jnp.concatenate([aff_c.reshape(1, E, NT), aff_l], axis=0)

        ff = _expert_ffn(u2, idx, aff, wg_bf, wu_bf, wd_bf, l)
        xa = _ffn_residual(x1, ff, g2, norm_post_ffn[l][None], 0, 1)
        xb = _ffn_residual(x1, ff, g2, norm_post_ffn[l][None], 1, NB)

    y_prompt = xa.reshape(B, SEQ, D)
    y_sample = xb
    new_cache_k = jnp.stack(new_k, axis=1)
    new_cache_v = jnp.stack(new_v, axis=1)
    new_state = jnp.stack(new_s, axis=1).astype(x_prompt.dtype)
    return (y_prompt, y_sample, new_cache_k, new_cache_v, new_state)
```
